```python
import jax
import jax.numpy as jnp
from jax import lax
import numpy as np

D_MODEL = 1024
BATCH = 8
SEQ = 4096
DEPTH = 4

FFN_DIM = 2816
NORM_EPS = 1e-6
A_HEADS = 8
A_HEAD_DIM = 64
A_WIDTH = A_HEADS * A_HEAD_DIM
ROT_DIM = A_HEAD_DIM // 4
ROPE_THETA = 500000.0
DILATED_PATTERNS = ((128, 1), (512, 4), (2048, 16))
WIN_BLOCK = 128
CONV_CHANNELS = D_MODEL - A_WIDTH
CONV_WIDTH = 31
HYB_IN = 3 * A_WIDTH + 2 * CONV_CHANNELS
GDN_HEADS = 8
GDN_KEY_DIM = 128
GDN_VALUE_DIM = 128
GDN_QK_W = GDN_HEADS * GDN_KEY_DIM
GDN_V_W = GDN_HEADS * GDN_VALUE_DIM
GDN_QKV_W = 2 * GDN_QK_W + GDN_V_W
GDN_SHORT_CONV = 4
GDN_CHUNK = 64
GDN_IN = GDN_QKV_W + GDN_V_W + 2 * GDN_HEADS
N_EVEN = (DEPTH + 1) // 2
N_ODD = DEPTH // 2

kernel_name = "hybrid_dilated_conformer_gdn_trunk"


def rms_norm(x, g):
    xf = x.astype(jnp.float32)
    y = xf * lax.rsqrt(jnp.mean(xf * xf, axis=-1, keepdims=True) + NORM_EPS)
    return (y * g.astype(jnp.float32)).astype(x.dtype)


def layer_norm(x, g, b):
    xf = x.astype(jnp.float32)
    mu = jnp.mean(xf, axis=-1, keepdims=True)
    xc = xf - mu
    y = xc * lax.rsqrt(jnp.mean(xc * xc, axis=-1, keepdims=True) + NORM_EPS)
    return (y * g.astype(jnp.float32) + b.astype(jnp.float32)).astype(x.dtype)


def l2_normalize(x):
    xf = x.astype(jnp.float32)
    return xf * lax.rsqrt(jnp.sum(xf * xf, axis=-1, keepdims=True) + NORM_EPS)


def swiglu_ffn(h, w_in, w_out):
    gate, up = jnp.split(h @ w_in, 2, axis=-1)
    return (jax.nn.silu(gate) * up) @ w_out


def causal_depthwise_conv(x, w):
    width, ch = w.shape
    return lax.conv_general_dilated(
        x, w[:, None, :].astype(x.dtype), window_strides=(1,), padding=[(width - 1, 0)],
        dimension_numbers=("NWC", "WIO", "NWC"), feature_group_count=ch)


def rotary_angles(positions):
    inv_freq = jnp.power(jnp.float32(ROPE_THETA),
                         -jnp.arange(0, ROT_DIM, 2, dtype=jnp.float32) / ROT_DIM)
    ang = positions.astype(jnp.float32)[..., None] * inv_freq
    return jnp.cos(ang)[:, :, None, :], jnp.sin(ang)[:, :, None, :]


def apply_partial_rotary(x, cos, sin):
    half = ROT_DIM // 2
    xr = x[..., :ROT_DIM].astype(jnp.float32)
    x1, x2 = xr[..., :half], xr[..., half:]
    rot = jnp.concatenate([x1 * cos - x2 * sin, x2 * cos + x1 * sin], axis=-1).astype(x.dtype)
    return jnp.concatenate([rot, x[..., ROT_DIM:]], axis=-1)


def dilated_window_branch(q, k, v, window, dilation):
    B, S, H, hd = q.shape
    span = window // dilation
    L = S // dilation
    nb = -(-L // WIN_BLOCK)
    Lp = nb * WIN_BLOCK

    def to_blocks(t):
        t = jnp.moveaxis(t.reshape(B, L, dilation, H, hd), 2, 1)
        t = jnp.pad(t, ((0, 0), (0, 0), (0, Lp - L), (0, 0), (0, 0)))
        return t.reshape(B, dilation, nb, WIN_BLOCK, H, hd)

    def with_prev(t):
        prev = jnp.pad(t, ((0, 0), (0, 0), (1, 0), (0, 0), (0, 0), (0, 0)))[:, :, :nb]
        return jnp.concatenate([prev, t], axis=3)

    qb = to_blocks(q)
    kb = with_prev(to_blocks(k))
    vb = with_prev(to_blocks(v))
    s = jnp.einsum("brnqhd,brnkhd->brnhqk", qb, kb).astype(jnp.float32) * (hd ** -0.5)
    blk = jnp.arange(nb)[:, None, None] * WIN_BLOCK
    q_idx = blk + jnp.arange(WIN_BLOCK)[None, :, None]
    k_idx = blk - WIN_BLOCK + jnp.arange(2 * WIN_BLOCK)[None, None, :]
    dist = q_idx - k_idx
    allowed = (dist >= 0) & (dist <= span) & (k_idx >= 0)
    s = jnp.where(allowed[:, None], s, -jnp.inf)
    m = jnp.max(s, axis=-1, keepdims=True)
    p = jnp.exp(s - m)
    den = jnp.sum(p, axis=-1, keepdims=True)
    o = jnp.einsum("brnhqk,brnkhd->brnqhd", (p / den).astype(v.dtype), vb)
    lse = jnp.swapaxes((m + jnp.log(den))[..., 0], -1, -2)

    def from_blocks(t):
        t = t.reshape((B, dilation, Lp) + t.shape[4:])[:, :, :L]
        return jnp.moveaxis(t, 1, 2).reshape((B, S) + t.shape[3:])

    return from_blocks(o), from_blocks(lse)


def hybrid_mixer(h, cos, sin, w_in, dw_w, dw_b, ln_g, ln_b, w_out):
    B, S, _ = h.shape
    proj = h @ w_in
    heads = lambda t: t.reshape(B, S, A_HEADS, A_HEAD_DIM)
    q = apply_partial_rotary(heads(proj[..., :A_WIDTH]), cos, sin)
    k = apply_partial_rotary(heads(proj[..., A_WIDTH:2 * A_WIDTH]), cos, sin)
    v = heads(proj[..., 2 * A_WIDTH:3 * A_WIDTH])
    outs, lses = [], []
    for window, dilation in DILATED_PATTERNS:
        o_g, lse_g = dilated_window_branch(q, k, v, window, dilation)
        outs.append(o_g)
        lses.append(lse_g)
    mix_w = jax.nn.softmax(jnp.stack(lses), axis=0)
    attn = jnp.einsum("gbsh,gbshd->bshd", mix_w, jnp.stack(outs).astype(jnp.float32))
    attn = attn.reshape(B, S, A_WIDTH).astype(h.dtype)
    u = proj[..., 3 * A_WIDTH:]
    glu = u[..., :CONV_CHANNELS] * jax.nn.sigmoid(u[..., CONV_CHANNELS:])
    c = causal_depthwise_conv(glu, dw_w) + dw_b
    c = jax.nn.silu(layer_norm(c, ln_g, ln_b))
    return jnp.concatenate([attn, c], axis=-1) @ w_out


def gated_delta_rule(q, k, v, g, beta):
    B, S, H, dk = q.shape
    C = GDN_CHUNK
    N = S // C

    def chunked(t):
        return jnp.moveaxis(t.reshape((B, N, C, H) + t.shape[3:]), 3, 1)

    q = chunked(q) * (dk ** -0.5)
    k = chunked(k)
    v = chunked(v)
    beta = chunked(beta)
    g = jnp.cumsum(chunked(g), axis=-1)
    causal = jnp.tril(jnp.ones((C, C), dtype=bool))
    strict = jnp.tril(jnp.ones((C, C), dtype=bool), -1)
    decay = jnp.exp(jnp.where(causal, g[..., :, None] - g[..., None, :], -jnp.inf))
    k_beta = k * beta[..., None]
    l_mat = jnp.where(strict, jnp.einsum("bhnik,bhnjk->bhnij", k_beta, k) * decay, 0.0)
    eye = jnp.eye(C, dtype=q.dtype)
    t_inv = lax.linalg.triangular_solve(l_mat + eye, jnp.broadcast_to(eye, l_mat.shape),
                                        left_side=True, lower=True, unit_diagonal=True)
    u = jnp.einsum("bhnij,bhnjv->bhniv", t_inv, v * beta[..., None])
    w = jnp.einsum("bhnij,bhnjk->bhnik", t_inv, k_beta * jnp.exp(g)[..., None])
    attn = jnp.where(causal, jnp.einsum("bhnik,bhnjk->bhnij", q, k) * decay, 0.0)
    q_dec = q * jnp.exp(g)[..., None]
    g_last = g[..., -1]
    k_dec = k * jnp.exp(g_last[..., None] - g)[..., None]
    xs = tuple(jnp.moveaxis(t, 2, 0) for t in (w, u, q_dec, k_dec, attn, g_last))

    def step(state, inp):
        w_c, u_c, q_c, k_c, a_c, gl_c = inp
        v_new = u_c - jnp.einsum("bhik,bhkv->bhiv", w_c, state)
        o_c = jnp.einsum("bhik,bhkv->bhiv", q_c, state) + jnp.einsum("bhij,bhjv->bhiv", a_c, v_new)
        state = state * jnp.exp(gl_c)[..., None, None] + jnp.einsum("bhik,bhiv->bhkv", k_c, v_new)
        return state, o_c

    state0 = jnp.zeros((B, H, dk, v.shape[-1]), q.dtype)
    _, o = lax.scan(step, state0, xs)
    return jnp.transpose(o, (1, 0, 3, 2, 4)).reshape(B, S, H, -1)


def gated_deltanet_mixer(h, w_in, conv_w, a_log, dt_bias, norm_g, w_out):
    B, S, _ = h.shape
    proj = h @ w_in
    qkv = jax.nn.silu(causal_depthwise_conv(proj[..., :GDN_QKV_W], conv_w))
    z = proj[..., GDN_QKV_W:GDN_QKV_W + GDN_V_W]
    b = proj[..., GDN_QKV_W + GDN_V_W:GDN_QKV_W + GDN_V_W + GDN_HEADS]
    a = proj[..., GDN_QKV_W + GDN_V_W + GDN_HEADS:]
    q = qkv[..., :GDN_QK_W].reshape(B, S, GDN_HEADS, GDN_KEY_DIM)
    k = qkv[..., GDN_QK_W:2 * GDN_QK_W].reshape(B, S, GDN_HEADS, GDN_KEY_DIM)
    v = qkv[..., 2 * GDN_QK_W:].reshape(B, S, GDN_HEADS, GDN_VALUE_DIM)
    beta = jax.nn.sigmoid(b.astype(jnp.float32))
    g = -jnp.exp(a_log.astype(jnp.float32)) * jax.nn.softplus(
        a.astype(jnp.float32) + dt_bias.astype(jnp.float32))
    o = gated_delta_rule(l2_normalize(q), l2_normalize(k), v.astype(jnp.float32), g, beta)
    o = rms_norm(o, norm_g) * jax.nn.silu(
        z.reshape(B, S, GDN_HEADS, GDN_VALUE_DIM).astype(jnp.float32))
    return o.reshape(B, S, GDN_V_W).astype(h.dtype) @ w_out


def setup_inputs(seed: int = 0) -> dict:
    key = jax.random.key(seed)
    ks = jax.random.split(key, 24)
    f32 = jnp.float32

    def dense(kk, shape, fan_in):
        return jax.random.normal(kk, shape, f32) * (fan_in ** -0.5)

    def gain(kk, shape):
        return 1.0 + 0.02 * jax.random.normal(kk, shape, f32)

    x = jax.random.normal(ks[0], (BATCH, SEQ, D_MODEL), f32)
    offsets = jax.random.randint(ks[1], (BATCH, 1), 0, 1024, dtype=jnp.int32)
    positions = offsets + jnp.arange(SEQ, dtype=jnp.int32)[None, :]
    dt = jnp.exp(jax.random.uniform(ks[18], (N_ODD, GDN_HEADS), f32,
                                    float(np.log(1e-3)), float(np.log(1e-1))))
    return {
        "x": x,
        "positions": positions,
        "ffn1_norm": gain(ks[2], (DEPTH, D_MODEL)),
        "ffn1_w_in": dense(ks[3], (DEPTH, D_MODEL, 2 * FFN_DIM), D_MODEL),
        "ffn1_w_out": dense(ks[4], (DEPTH, FFN_DIM, D_MODEL), FFN_DIM),
        "mix_norm": gain(ks[5], (DEPTH, D_MODEL)),
        "ffn2_norm": gain(ks[6], (DEPTH, D_MODEL)),
        "ffn2_w_in": dense(ks[7], (DEPTH, D_MODEL, 2 * FFN_DIM), D_MODEL),
        "ffn2_w_out": dense(ks[8], (DEPTH, FFN_DIM, D_MODEL), FFN_DIM),
        "hyb_w_in": dense(ks[9], (N_EVEN, D_MODEL, HYB_IN), D_MODEL),
        "hyb_dw_w": dense(ks[10], (N_EVEN, CONV_WIDTH, CONV_CHANNELS), CONV_WIDTH),
        "hyb_dw_b": 0.02 * jax.random.normal(ks[11], (N_EVEN, CONV_CHANNELS), f32),
        "hyb_ln_g": gain(ks[12], (N_EVEN, CONV_CHANNELS)),
        "hyb_ln_b": 0.02 * jax.random.normal(ks[13], (N_EVEN, CONV_CHANNELS), f32),
        "hyb_w_out": dense(ks[14], (N_EVEN, A_WIDTH + CONV_CHANNELS, D_MODEL), A_WIDTH + CONV_CHANNELS),
        "gdn_w_in": dense(ks[15], (N_ODD, D_MODEL, GDN_IN), D_MODEL),
        "gdn_conv_w": dense(ks[16], (N_ODD, GDN_SHORT_CONV, GDN_QKV_W), GDN_SHORT_CONV),
        "gdn_A_log": jnp.log(jax.random.uniform(ks[17], (N_ODD, GDN_HEADS), f32, 1.0, 16.0)),
        "gdn_dt_bias": dt + jnp.log(-jnp.expm1(-dt)),
        "gdn_norm_g": gain(ks[19], (N_ODD, GDN_VALUE_DIM)),
        "gdn_w_out": dense(ks[20], (N_ODD, GDN_V_W, D_MODEL), GDN_V_W),
        "final_norm": gain(ks[21], (D_MODEL,)),
    }


def reference(x, positions, ffn1_norm, ffn1_w_in, ffn1_w_out, mix_norm, ffn2_norm, ffn2_w_in,
              ffn2_w_out, hyb_w_in, hyb_dw_w, hyb_dw_b, hyb_ln_g, hyb_ln_b, hyb_w_out,
              gdn_w_in, gdn_conv_w, gdn_A_log, gdn_dt_bias, gdn_norm_g, gdn_w_out, final_norm):
    cos, sin = rotary_angles(positions)
    h = x
    for layer in range(DEPTH):
        h = h + 0.5 * swiglu_ffn(rms_norm(h, ffn1_norm[layer]), ffn1_w_in[layer], ffn1_w_out[layer])
        hn = rms_norm(h, mix_norm[layer])
        i = layer // 2
        if layer % 2 == 0:
            mix = hybrid_mixer(hn, cos, sin, hyb_w_in[i], hyb_dw_w[i], hyb_dw_b[i],
                               hyb_ln_g[i], hyb_ln_b[i], hyb_w_out[i])
        else:
            mix = gated_deltanet_mixer(hn, gdn_w_in[i], gdn_conv_w[i], gdn_A_log[i],
                                       gdn_dt_bias[i], gdn_norm_g[i], gdn_w_out[i])
        h = h + mix
        h = h + 0.5 * swiglu_ffn(rms_norm(h, ffn2_norm[layer]), ffn2_w_in[layer], ffn2_w_out[layer])
    return rms_norm(h, final_norm)
```

```python
import functools

import jax
import jax.numpy as jnp
from jax import lax
from jax.experimental import pallas as pl
from jax.experimental.pallas import tpu as pltpu

F32 = jnp.float32
BF16 = jnp.bfloat16

NORM_EPS = 1e-6
ROPE_THETA = 500000.0
A_HEADS = 8
A_HEAD_DIM = 64
A_WIDTH = A_HEADS * A_HEAD_DIM
ROT_DIM = A_HEAD_DIM // 4
DILATIONS = (1, 4, 16)
WIN_BLOCK = 128
CONV_WIDTH = 31
GDN_HEADS = 8
GDN_DIM = 128
GDN_CHUNK = 64
GDN_SHORT_CONV = 4

LANES = 128
SUBLANES = 8
MXU_DIM = 256
VMEM_LIMIT = 56 * 1024 * 1024
MASK_VALUE = -1e30


def _params(n_axes):
    return pltpu.CompilerParams(dimension_semantics=("arbitrary",) * n_axes,
                                vmem_limit_bytes=VMEM_LIMIT)


def _resident(shape):
    nd = len(shape)
    return pl.BlockSpec(shape, lambda *_: (0,) * nd, pipeline_mode=pl.Buffered(1))


def _rms_bf16(x, g):
    ms = jnp.mean(x * x, axis=-1, keepdims=True)
    return (x * lax.rsqrt(ms + NORM_EPS) * g).astype(BF16)


def _silu(x):
    return x * jax.nn.sigmoid(x)


def _dot(a, b):
    return jnp.dot(a, b, preferred_element_type=F32)


def _dot_nt(a, b):
    return lax.dot_general(a, b, (((1,), (1,)), ((), ())), preferred_element_type=F32)


def _dot_tn(a, b):
    return lax.dot_general(a, b, (((0,), (0,)), ((), ())), preferred_element_type=F32)


def _ffn_body(h_ref, g_ref, win_ref, wout_ref, fg_ref, o_ref, xn_ref, acc_ref, *, ffn, chunks,
              final_norm):
    xn_ref[...] = _rms_bf16(h_ref[...], g_ref[...])
    for n, (c0, cw) in enumerate(chunks):
        xn = xn_ref[...]
        gate = _dot(xn, win_ref[:, c0:c0 + cw])
        up = _dot(xn, win_ref[:, ffn + c0:ffn + c0 + cw])
        part = _dot((_silu(gate) * up).astype(BF16), wout_ref[c0:c0 + cw, :])
        if n == 0:
            acc_ref[...] = part
        else:
            acc_ref[...] += part
    out = h_ref[...] + 0.5 * acc_ref[...]
    if final_norm:
        ms = jnp.mean(out * out, axis=-1, keepdims=True)
        out = out * lax.rsqrt(ms + NORM_EPS) * fg_ref[...]
    o_ref[...] = out


def _ffn_chunks(ffn, width):
    out, c0 = [], 0
    while c0 < ffn:
        cw = min(width, ffn - c0)
        out.append((c0, cw))
        c0 += cw
    return tuple(out)


def _ffn(h, g, w_in, w_out, final_g, *, final_norm, tm):
    t, d = h.shape
    ffn = w_out.shape[0]
    body = functools.partial(_ffn_body, ffn=ffn, chunks=_ffn_chunks(ffn, 2 * MXU_DIM),
                             final_norm=final_norm)
    return pl.pallas_call(
        body,
        grid=(t // tm,),
        in_specs=[pl.BlockSpec((tm, d), lambda i: (i, 0)),
                  _resident((1, d)), _resident(w_in.shape), _resident(w_out.shape),
                  _resident((1, d))],
        out_specs=pl.BlockSpec((tm, d), lambda i: (i, 0)),
        out_shape=jax.ShapeDtypeStruct((t, d), F32),
        scratch_shapes=[pltpu.VMEM((tm, d), BF16), pltpu.VMEM((tm, d), F32)],
        compiler_params=_params(1),
        name="ffn_final" if final_norm else "ffn",
    )(h, g, w_in, w_out, final_g)


def _hyb_in_body(h_ref, g_ref, w_ref, pos_ref, invf_ref, shi_ref, slo_ref,
                 q_ref, k_ref, v_ref, glu_ref):
    xn = _rms_bf16(h_ref[...], g_ref[...])
    proj = _dot(xn, w_ref[...])
    ang = pos_ref[...].astype(F32) * invf_ref[...]
    cos, sin = jnp.cos(ang), jnp.sin(ang)
    reps = A_WIDTH // LANES
    cos4 = jnp.concatenate([cos] * reps, axis=1)
    sin_hi = jnp.concatenate([sin * shi_ref[...]] * reps, axis=1)
    sin_lo = jnp.concatenate([sin * slo_ref[...]] * reps, axis=1)
    half = ROT_DIM // 2

    def rot(x):
        return (x * cos4 + pltpu.roll(x, half, 1) * sin_hi
                + pltpu.roll(x, A_WIDTH - half, 1) * sin_lo)

    q_ref[...] = (rot(proj[:, :A_WIDTH]) * (A_HEAD_DIM ** -0.5)).astype(BF16)
    k_ref[...] = rot(proj[:, A_WIDTH:2 * A_WIDTH]).astype(BF16)
    v_ref[...] = proj[:, 2 * A_WIDTH:3 * A_WIDTH].astype(BF16)
    u = proj[:, 3 * A_WIDTH:]
    c = u.shape[1] // 2
    glu_ref[...] = u[:, :c] * jax.nn.sigmoid(u[:, c:])


def _hyb_in(h, g, w, pos, invf, shi, slo, *, tm):
    t, d = h.shape
    cc = (w.shape[1] - 3 * A_WIDTH) // 2
    row = lambda width: pl.BlockSpec((tm, width), lambda i: (i, 0))
    return pl.pallas_call(
        _hyb_in_body,
        grid=(t // tm,),
        in_specs=[row(d), _resident((1, d)), _resident(w.shape), row(1),
                  _resident((1, LANES)), _resident((1, LANES)), _resident((1, LANES))],
        out_specs=[row(A_WIDTH), row(A_WIDTH), row(A_WIDTH), row(cc)],
        out_shape=[jax.ShapeDtypeStruct((t, A_WIDTH), BF16)] * 3
        + [jax.ShapeDtypeStruct((t, cc), F32)],
        compiler_params=_params(1),
        name="hyb_in",
    )(h, g, w, pos, invf, shi, slo)


def _attn_body(q_ref, kc_ref, kp_ref, vc_ref, vp_ref, bias_ref, o_ref, lse_ref, kcat, vcat, *, tq):
    i = pl.program_id(2)
    w = WIN_BLOCK
    kcat[0:w, :] = kp_ref[0]
    kcat[w:, :] = kc_ref[0]
    vcat[0:w, :] = vp_ref[0]
    vcat[w:, :] = vc_ref[0]
    slab = MXU_DIM
    hps = slab // A_HEAD_DIM
    lane_head = lax.broadcasted_iota(jnp.int32, (1, slab), 1) // A_HEAD_DIM
    lane_idx = lax.broadcasted_iota(jnp.int32, (1, LANES), 1)
    for j in range(tq // w):
        first = jnp.where(jnp.logical_and(i == 0, j == 0), 1, 0) if j == 0 else 0
        bias = bias_ref[first]
        lse_tile = jnp.zeros((w, LANES), F32)
        for s in range(A_WIDTH // slab):
            cols = slice(s * slab, (s + 1) * slab)
            qs = q_ref[0, j * w:(j + 1) * w, cols]
            lhs = jnp.concatenate(
                [jnp.where(lane_head == hh, qs, jnp.zeros_like(qs)) for hh in range(hps)], axis=0)
            sc = _dot_nt(lhs, kcat[j * w:(j + 2) * w, cols]) + bias
            m = jnp.max(sc, axis=1, keepdims=True)
            p = jnp.exp(sc - m)
            den = jnp.sum(p, axis=1, keepdims=True)
            pv = _dot(p.astype(BF16), vcat[j * w:(j + 2) * w, cols]) / den
            lse = m + jnp.log(den)
            o_slab = jnp.zeros((w, slab), F32)
            for hh in range(hps):
                rows = slice(hh * w, (hh + 1) * w)
                o_slab = o_slab + jnp.where(lane_head == hh, pv[rows], 0.0)
                lse_tile = lse_tile + jnp.where(lane_idx == s * hps + hh, lse[rows], 0.0)
            o_ref[0, j * w:(j + 1) * w, cols] = o_slab
        lse_ref[0, j * w:(j + 1) * w, :] = lse_tile


def _attn_bias():
    w = WIN_BLOCK
    r = jnp.arange(w)[:, None]
    c = jnp.arange(2 * w)[None, :]
    band = (c >= r) & (c <= r + w)
    no_prev = band & (c >= w)
    bias = jnp.where(jnp.stack([band, no_prev]), 0.0, MASK_VALUE).astype(F32)
    return jnp.tile(bias, (1, MXU_DIM // A_HEAD_DIM, 1))


def _attn_branch(q, k, v, bias, dil):
    b, s, width = q.shape
    l = s // dil
    w = WIN_BLOCK
    tq = min(4 * w, l)
    view = lambda a: a.reshape(b, l, dil * width)
    cur = pl.BlockSpec((1, tq, width), lambda bi, c, i: (bi, i, c))
    prev = pl.BlockSpec((1, w, width), lambda bi, c, i: (bi, jnp.maximum(i * (tq // w) - 1, 0), c))
    o, lse = pl.pallas_call(
        functools.partial(_attn_body, tq=tq),
        grid=(b, dil, l // tq),
        in_specs=[cur, cur, prev, cur, prev, _resident(bias.shape)],
        out_specs=[cur, pl.BlockSpec((1, tq, LANES), lambda bi, c, i: (bi, i, c))],
        out_shape=[jax.ShapeDtypeStruct((b, l, dil * width), F32),
                   jax.ShapeDtypeStruct((b, l, dil * LANES), F32)],
        scratch_shapes=[pltpu.VMEM((tq + w, width), BF16), pltpu.VMEM((tq + w, width), BF16)],
        compiler_params=_params(3),
        name=f"attn_d{dil}",
    )(view(q), view(k), view(k), view(v), view(v), bias)
    return o.reshape(b, s, width), lse.reshape(b, s, LANES)


def _hyb_out_body(h_ref, o1_ref, o2_ref, o3_ref, l1_ref, l2_ref, l3_ref, glu_ref, gprev_ref,
                  dww_ref, dwb_ref, lng_ref, lnb_ref, wout_ref, out_ref, xpad, *, tm, halo,
                  tiles_per_seq):
    i = pl.program_id(0)
    l1, l2, l3 = l1_ref[...], l2_ref[...], l3_ref[...]
    mx = jnp.maximum(jnp.maximum(l1, l2), l3)
    e1, e2, e3 = jnp.exp(l1 - mx), jnp.exp(l2 - mx), jnp.exp(l3 - mx)
    tot = e1 + e2 + e3
    ws = (e1 / tot, e2 / tot, e3 / tot)
    lane = lax.broadcasted_iota(jnp.int32, (1, LANES), 1)
    heads_per_vreg = LANES // A_HEAD_DIM
    attn = []
    for s in range(A_WIDTH // LANES):
        cols = slice(s * LANES, (s + 1) * LANES)
        acc = None
        for wg, o_ref in zip(ws, (o1_ref, o2_ref, o3_ref)):
            wfull = wg[:, s * heads_per_vreg:s * heads_per_vreg + 1]
            for hh in range(1, heads_per_vreg):
                col = s * heads_per_vreg + hh
                wfull = jnp.where(lane < hh * A_HEAD_DIM, wfull, wg[:, col:col + 1])
            term = wfull * o_ref[:, cols]
            acc = term if acc is None else acc + term
        attn.append(acc)
    attn = jnp.concatenate(attn, axis=1)

    prev = gprev_ref[...]
    xpad[0:halo, :] = jnp.where(i % tiles_per_seq == 0, jnp.zeros_like(prev), prev)
    xpad[halo:, :] = glu_ref[...]
    conv = None
    for j in range(CONV_WIDTH):
        off = halo - (CONV_WIDTH - 1) + j
        term = xpad[off:off + tm, :] * dww_ref[j:j + 1, :]
        conv = term if conv is None else conv + term
    conv = conv + dwb_ref[...]
    mu = jnp.mean(conv, axis=-1, keepdims=True)
    xc = conv - mu
    y = xc * lax.rsqrt(jnp.mean(xc * xc, axis=-1, keepdims=True) + NORM_EPS)
    c = _silu(y * lng_ref[...] + lnb_ref[...])

    cat = jnp.concatenate([attn, c], axis=1).astype(BF16)
    out_ref[...] = h_ref[...] + _dot(cat, wout_ref[...])


def _hyb_out(h, os_, lses, glu, dw_w, dw_b, ln_g, ln_b, w_out, *, tm, seq):
    t, d = h.shape
    cc = glu.shape[1]
    halo = 4 * SUBLANES
    row = lambda width: pl.BlockSpec((tm, width), lambda i: (i, 0))
    prev = pl.BlockSpec((halo, cc), lambda i: (jnp.maximum(i * (tm // halo) - 1, 0), 0))
    body = functools.partial(_hyb_out_body, tm=tm, halo=halo, tiles_per_seq=seq // tm)
    return pl.pallas_call(
        body,
        grid=(t // tm,),
        in_specs=[row(d)] + [row(A_WIDTH)] * 3 + [row(LANES)] * 3 + [row(cc), prev,
                  _resident(dw_w.shape), _resident((1, cc)), _resident((1, cc)),
                  _resident((1, cc)), _resident(w_out.shape)],
        out_specs=row(d),
        out_shape=jax.ShapeDtypeStruct((t, d), F32),
        scratch_shapes=[pltpu.VMEM((tm + halo, cc), F32)],
        compiler_params=_params(1),
        name="hyb_out",
    )(h, *os_, *lses, glu, glu, dw_w, dw_b, ln_g, ln_b, w_out)


def _gdn_in_body(h_ref, g_ref, w_ref, ws_ref, cw_ref, alog_ref, dtb_ref,
                 q_ref, k_ref, v_ref, z_ref, beta_ref, gl_ref, xpad, *, tm, tiles_per_seq):
    i = pl.program_id(0)
    xn = _rms_bf16(h_ref[...], g_ref[...])
    proj = _dot(xn, w_ref[...])
    small = _dot(xn, ws_ref[...])
    hw = GDN_HEADS * GDN_DIM
    halo = SUBLANES

    @pl.when(i % tiles_per_seq == 0)
    def _():
        xpad[0:halo, :] = jnp.zeros((halo, 3 * hw), F32)

    @pl.when(i % tiles_per_seq != 0)
    def _():
        xpad[0:halo, :] = xpad[tm:tm + halo, :]

    xpad[halo:, :] = proj[:, :3 * hw]
    conv = None
    for j in range(GDN_SHORT_CONV):
        off = halo - (GDN_SHORT_CONV - 1) + j
        term = xpad[off:off + tm, :] * cw_ref[j:j + 1, :]
        conv = term if conv is None else conv + term
    qkv = _silu(conv)

    def l2n(x):
        return x * lax.rsqrt(jnp.sum(x * x, axis=-1, keepdims=True) + NORM_EPS)

    for hh in range(GDN_HEADS):
        cols = slice(hh * GDN_DIM, (hh + 1) * GDN_DIM)
        q_ref[hh] = l2n(qkv[:, cols])
        k_ref[hh] = l2n(qkv[:, hw:2 * hw][:, cols])
        v_ref[hh] = qkv[:, 2 * hw:][:, cols]
        z_ref[hh] = proj[:, 3 * hw:][:, cols]

    beta = jax.nn.sigmoid(small)
    x = small + dtb_ref[...]
    softplus = jnp.maximum(x, 0.0) + jnp.log1p(jnp.exp(-jnp.abs(x)))
    gl = -jnp.exp(alog_ref[...]) * softplus
    beta_t = beta.T
    gl_t = gl.T
    for hh in range(GDN_HEADS):
        beta_ref[hh] = beta_t[hh:hh + 1, :]
        gl_ref[hh] = gl_t[GDN_HEADS + hh:GDN_HEADS + hh + 1, :]


def _gdn_in(h, g, w_main, w_small, conv_w, alog, dtb, *, tm, seq):
    t, d = h.shape
    hw = GDN_HEADS * GDN_DIM
    head_major = pl.BlockSpec((GDN_HEADS, tm, GDN_DIM), lambda i: (0, i, 0))
    rows = pl.BlockSpec((GDN_HEADS, 1, tm), lambda i: (0, 0, i))
    body = functools.partial(_gdn_in_body, tm=tm, tiles_per_seq=seq // tm)
    return pl.pallas_call(
        body,
        grid=(t // tm,),
        in_specs=[pl.BlockSpec((tm, d), lambda i: (i, 0)), _resident((1, d)),
                  _resident(w_main.shape), _resident(w_small.shape), _resident(conv_w.shape),
                  _resident((1, LANES)), _resident((1, LANES))],
        out_specs=[head_major] * 4 + [rows] * 2,
        out_shape=[jax.ShapeDtypeStruct((GDN_HEADS, t, GDN_DIM), F32)] * 4
        + [jax.ShapeDtypeStruct((GDN_HEADS, 1, t), F32)] * 2,
        scratch_shapes=[pltpu.VMEM((tm + SUBLANES, 3 * hw), F32)],
        compiler_params=_params(1),
        name="gdn_in",
    )(h, g, w_main, w_small, conv_w, alog, dtb)


def _split2(a):
    hi = a.astype(BF16)
    lo = (a - hi.astype(F32)).astype(BF16)
    return hi, lo


def _mm_f32(a, b):
    a_hi, a_lo = _split2(a)
    b_hi, b_lo = _split2(b)
    lhs = jnp.concatenate([a_hi, a_hi, a_lo, a_lo], axis=1)
    rhs = jnp.concatenate([b_hi, b_lo, b_hi, b_lo], axis=0)
    return _dot(lhs, rhs)


def _unit_lower_inverse(lm, eye, nilpotency):
    t = eye - lm
    p = lm
    n = 1
    while n < nilpotency // 2:
        p = _mm_f32(p, p)
        t = t + _mm_f32(t, p)
        n *= 2
    return t


def _gdn_core_body(q_ref, k_ref, v_ref, z_ref, beta_ref, gl_ref, ng_ref, o_ref, state, *, hb, tc):
    n = pl.program_id(2)
    c = GDN_CHUNK
    pr = 2 * c

    @pl.when(n == 0)
    def _():
        state[...] = jnp.zeros_like(state)

    ri = lax.broadcasted_iota(jnp.int32, (pr, pr), 0)
    ci = lax.broadcasted_iota(jnp.int32, (pr, pr), 1)
    same_chunk = (ri // c) == (ci // c)
    eye = ri == ci
    eye_f = eye.astype(F32)
    causal = jnp.logical_and(same_chunk, ri >= ci)
    strict = jnp.logical_and(same_chunk, ri > ci)
    tri_u = jnp.logical_and(same_chunk, ri <= ci).astype(BF16)
    tri_u3 = jnp.concatenate([tri_u] * 3, axis=0)
    lane = lax.broadcasted_iota(jnp.int32, (1, pr), 1)
    scale = GDN_DIM ** -0.5
    rows16 = 2 * SUBLANES

    def to_col(row):
        return jnp.sum(jnp.where(eye, row, 0.0), axis=1, keepdims=True)

    for hh in range(hb):
        for pp in range(tc // pr):
            rows = slice(pp * pr, (pp + 1) * pr)
            q = q_ref[hh, rows, :] * scale
            k = k_ref[hh, rows, :]
            v = v_ref[hh, rows, :]
            beta_row = beta_ref[hh, :, rows]
            gl_row = gl_ref[hh, :, rows]
            g1 = gl_row.astype(BF16)
            r1 = gl_row - g1.astype(F32)
            g2 = r1.astype(BF16)
            g3 = (r1 - g2.astype(F32)).astype(BF16)
            g_terms = jnp.broadcast_to(jnp.concatenate([g1, g2, g3], axis=1), (rows16, 3 * pr))
            g_row = _dot(g_terms, tri_u3)[0:1, :]
            g_col = to_col(g_row)
            b_col = to_col(beta_row)
            decay = jnp.exp(jnp.where(causal, g_col - g_row, MASK_VALUE))
            eg = jnp.exp(g_col)
            kb = k * b_col
            kk = k.astype(BF16)
            p1 = _dot_nt(jnp.concatenate([kb, q], axis=0).astype(BF16), kk)
            lm = jnp.where(strict, p1[:pr] * decay, 0.0)
            attn = (p1[pr:] * decay).astype(BF16)
            t_inv = _unit_lower_inverse(lm, eye_f, c)
            p2 = _dot(t_inv.astype(BF16),
                      jnp.concatenate([v * b_col, kb * eg], axis=1).astype(BF16))
            u, w = p2[:, :GDN_DIM], p2[:, GDN_DIM:]
            q_dec = q * eg
            v_news = []
            for cc in range(2):
                cr = slice(cc * c, (cc + 1) * c)
                s_old = state[hh]
                p3 = _dot(jnp.concatenate([w[cr], q_dec[cr]], axis=0).astype(BF16),
                          s_old.astype(BF16))
                v_new = u[cr] - p3[:c]
                v_news.append(v_new)
                v_pair = jnp.concatenate(v_news + [u[c:]] * (1 - cc), axis=0).astype(BF16)
                o = p3[c:] + _dot(attn[cr], v_pair)
                g_last = jnp.sum(jnp.where(lane == cc * c + c - 1, g_row, 0.0),
                                 axis=1, keepdims=True)
                k_dec = k[cr] * jnp.exp(g_last - g_col[cr])
                state[hh] = s_old * jnp.exp(g_last) + _dot_tn(k_dec.astype(BF16),
                                                              v_new.astype(BF16))
                on = (o * lax.rsqrt(jnp.mean(o * o, axis=-1, keepdims=True) + NORM_EPS)
                      * ng_ref[...])
                orow = slice(pp * pr + cc * c, pp * pr + (cc + 1) * c)
                o_ref[hh, orow, :] = (on * _silu(z_ref[hh, orow, :])).astype(BF16)


def _gdn_core(q, k, v, z, beta, gl, norm_g, *, batch, seq, hb, tc):
    heads, t, dim = q.shape
    tiles = seq // tc
    blk = pl.BlockSpec((hb, tc, dim), lambda hi, b, n: (hi, b * tiles + n, 0))
    rows = pl.BlockSpec((hb, 1, tc), lambda hi, b, n: (hi, 0, b * tiles + n))
    body = functools.partial(_gdn_core_body, hb=hb, tc=tc)
    return pl.pallas_call(
        body,
        grid=(heads // hb, batch, tiles),
        in_specs=[blk] * 4 + [rows] * 2 + [_resident((1, dim))],
        out_specs=blk,
        out_shape=jax.ShapeDtypeStruct((heads, t, dim), BF16),
        scratch_shapes=[pltpu.VMEM((hb, dim, dim), F32)],
        compiler_params=_params(3),
        name="gdn_core",
    )(q, k, v, z, beta, gl, norm_g)


def _gdn_out_body(h_ref, o_ref, w_ref, out_ref):
    o = jnp.concatenate([o_ref[hh] for hh in range(GDN_HEADS)], axis=1)
    out_ref[...] = h_ref[...] + _dot(o, w_ref[...])


def _gdn_out(h, o, w_out, *, tm):
    t, d = h.shape
    return pl.pallas_call(
        _gdn_out_body,
        grid=(t // tm,),
        in_specs=[pl.BlockSpec((tm, d), lambda i: (i, 0)),
                  pl.BlockSpec((GDN_HEADS, tm, GDN_DIM), lambda i: (0, i, 0)),
                  _resident(w_out.shape)],
        out_specs=pl.BlockSpec((tm, d), lambda i: (i, 0)),
        out_shape=jax.ShapeDtypeStruct((t, d), F32),
        compiler_params=_params(1),
        name="gdn_out",
    )(h, o, w_out)


def _tile(n, want):
    t = min(want, n)
    assert n % t == 0, (n, t)
    return t


def kernel(x, positions, ffn1_norm, ffn1_w_in, ffn1_w_out, mix_norm, ffn2_norm, ffn2_w_in,
           ffn2_w_out, hyb_w_in, hyb_dw_w, hyb_dw_b, hyb_ln_g, hyb_ln_b, hyb_w_out,
           gdn_w_in, gdn_conv_w, gdn_A_log, gdn_dt_bias, gdn_norm_g, gdn_w_out, final_norm):
    batch, seq, d = x.shape
    depth = ffn1_norm.shape[0]
    t = batch * seq
    h = x.reshape(t, d)
    pos = positions.reshape(t, 1)
    row = lambda a: a.reshape(1, -1).astype(F32)

    lane = jnp.arange(LANES) % A_HEAD_DIM
    inv_freq = jnp.power(jnp.float32(ROPE_THETA),
                         -jnp.arange(0, ROT_DIM, 2, dtype=F32) / ROT_DIM)
    half = ROT_DIM // 2
    invf = jnp.where(lane < ROT_DIM, inv_freq[lane % half], 0.0).astype(F32).reshape(1, LANES)
    shi = jnp.where((lane >= half) & (lane < ROT_DIM), 1.0, 0.0).astype(F32).reshape(1, LANES)
    slo = jnp.where(lane < half, -1.0, 0.0).astype(F32).reshape(1, LANES)
    bias = _attn_bias()

    gdn_hw = GDN_HEADS * GDN_DIM
    tm_ffn = _tile(t, 1024)
    tm_mix = _tile(seq, 512)
    final_g = row(final_norm)

    for layer in range(depth):
        i = layer // 2
        h = _ffn(h, row(ffn1_norm[layer]), ffn1_w_in[layer].astype(BF16),
                 ffn1_w_out[layer].astype(BF16), final_g, final_norm=False, tm=tm_ffn)
        if layer % 2 == 0:
            q, k, v, glu = _hyb_in(h, row(mix_norm[layer]), hyb_w_in[i].astype(BF16), pos,
                                   invf, shi, slo, tm=tm_mix)
            shape3 = (batch, seq, A_WIDTH)
            outs, lses = [], []
            for dil in DILATIONS:
                o_g, lse_g = _attn_branch(q.reshape(shape3), k.reshape(shape3),
                                          v.reshape(shape3), bias, dil)
                outs.append(o_g.reshape(t, A_WIDTH))
                lses.append(lse_g.reshape(t, LANES))
            h = _hyb_out(h, outs, lses, glu, hyb_dw_w[i], row(hyb_dw_b[i]), row(hyb_ln_g[i]),
                         row(hyb_ln_b[i]), hyb_w_out[i].astype(BF16), tm=tm_mix, seq=seq)
        else:
            w_in = gdn_w_in[i]
            w_main = w_in[:, :4 * gdn_hw].astype(BF16)
            w_small = jnp.pad(w_in[:, 4 * gdn_hw:],
                              ((0, 0), (0, LANES - 2 * GDN_HEADS))).astype(BF16)
            lane_pad = lambda a: jnp.pad(a.astype(F32), (GDN_HEADS, LANES - 2 * GDN_HEADS)
                                         ).reshape(1, LANES)
            q, k, v, z, beta, gl = _gdn_in(h, row(mix_norm[layer]), w_main, w_small,
                                           gdn_conv_w[i], lane_pad(gdn_A_log[i]),
                                           lane_pad(gdn_dt_bias[i]), tm=tm_mix, seq=seq)
            o = _gdn_core(q, k, v, z, beta, gl, row(gdn_norm_g[i]), batch=batch, seq=seq,
                          hb=2, tc=_tile(seq, 256))
            h = _gdn_out(h, o, gdn_w_out[i].astype(BF16), tm=tm_ffn)
        h = _ffn(h, row(ffn2_norm[layer]), ffn2_w_in[layer].astype(BF16),
                 ffn2_w_out[layer].astype(BF16), final_g,
                 final_norm=(layer == depth - 1), tm=tm_ffn)
    return h.reshape(batch, seq, d)
```

```python
import functools

import jax
import jax.numpy as jnp
from jax import lax
from jax.experimental import pallas as pl
from jax.experimental.pallas import tpu as pltpu

F32 = jnp.float32
BF16 = jnp.bfloat16

NORM_EPS = 1e-6
ROPE_THETA = 500000.0
A_HEADS = 8
A_HEAD_DIM = 64
A_WIDTH = A_HEADS * A_HEAD_DIM
ROT_DIM = A_HEAD_DIM // 4
DILATIONS = (1, 4, 16)
WIN_BLOCK = 128
CONV_WIDTH = 31
GDN_HEADS = 8
GDN_DIM = 128
GDN_CHUNK = 64
GDN_SHORT_CONV = 4

LANES = 128
SUBLANES = 8
MXU_DIM = 256
VMEM_LIMIT = 56 * 1024 * 1024
MASK_VALUE = -1e30


def _params(n_axes):
    return pltpu.CompilerParams(dimension_semantics=("arbitrary",) * n_axes,
                                vmem_limit_bytes=VMEM_LIMIT)


def _resident(shape):
    nd = len(shape)
    return pl.BlockSpec(shape, lambda *_: (0,) * nd, pipeline_mode=pl.Buffered(1))


def _rms_bf16(x, g):
    ms = jnp.mean(x * x, axis=-1, keepdims=True)
    return (x * lax.rsqrt(ms + NORM_EPS) * g).astype(BF16)


def _silu(x):
    return x * jax.nn.sigmoid(x)


def _dot(a, b):
    return jnp.dot(a, b, preferred_element_type=F32)


def _dot_nt(a, b):
    return lax.dot_general(a, b, (((1,), (1,)), ((), ())), preferred_element_type=F32)


def _dot_tn(a, b):
    return lax.dot_general(a, b, (((0,), (0,)), ((), ())), preferred_element_type=F32)


def _ffn_body(h_ref, g_ref, win_ref, wout_ref, fg_ref, o_ref, xn_ref, acc_ref, *, ffn, chunks,
              final_norm):
    xn_ref[...] = _rms_bf16(h_ref[...], g_ref[...])
    for n, (c0, cw) in enumerate(chunks):
        xn = xn_ref[...]
        gate = _dot(xn, win_ref[:, c0:c0 + cw])
        up = _dot(xn, win_ref[:, ffn + c0:ffn + c0 + cw])
        part = _dot((_silu(gate) * up).astype(BF16), wout_ref[c0:c0 + cw, :])
        if n == 0:
            acc_ref[...] = part
        else:
            acc_ref[...] += part
    out = h_ref[...] + 0.5 * acc_ref[...]
    if final_norm:
        ms = jnp.mean(out * out, axis=-1, keepdims=True)
        out = out * lax.rsqrt(ms + NORM_EPS) * fg_ref[...]
    o_ref[...] = out


def _ffn_chunks(ffn, width):
    out, c0 = [], 0
    while c0 < ffn:
        cw = min(width, ffn - c0)
        out.append((c0, cw))
        c0 += cw
    return tuple(out)


def _ffn(h, g, w_in, w_out, final_g, *, final_norm, tm):
    t, d = h.shape
    ffn = w_out.shape[0]
    body = functools.partial(_ffn_body, ffn=ffn, chunks=_ffn_chunks(ffn, 2 * MXU_DIM),
                             final_norm=final_norm)
    return pl.pallas_call(
        body,
        grid=(t // tm,),
        in_specs=[pl.BlockSpec((tm, d), lambda i: (i, 0)),
                  _resident((1, d)), _resident(w_in.shape), _resident(w_out.shape),
                  _resident((1, d))],
        out_specs=pl.BlockSpec((tm, d), lambda i: (i, 0)),
        out_shape=jax.ShapeDtypeStruct((t, d), F32),
        scratch_shapes=[pltpu.VMEM((tm, d), BF16), pltpu.VMEM((tm, d), F32)],
        compiler_params=_params(1),
        name="ffn_final" if final_norm else "ffn",
    )(h, g, w_in, w_out, final_g)


def _hyb_in_body(h_ref, g_ref, w_ref, pos_ref, invf_ref, shi_ref, slo_ref,
                 q_ref, k_ref, v_ref, glu_ref):
    xn = _rms_bf16(h_ref[...], g_ref[...])
    proj = _dot(xn, w_ref[...])
    ang = pos_ref[...].astype(F32) * invf_ref[...]
    cos, sin = jnp.cos(ang), jnp.sin(ang)
    reps = A_WIDTH // LANES
    cos4 = jnp.concatenate([cos] * reps, axis=1)
    sin_hi = jnp.concatenate([sin * shi_ref[...]] * reps, axis=1)
    sin_lo = jnp.concatenate([sin * slo_ref[...]] * reps, axis=1)
    half = ROT_DIM // 2

    def rot(x):
        return (x * cos4 + pltpu.roll(x, half, 1) * sin_hi
                + pltpu.roll(x, A_WIDTH - half, 1) * sin_lo)

    q_ref[...] = (rot(proj[:, :A_WIDTH]) * (A_HEAD_DIM ** -0.5)).astype(BF16)
    k_ref[...] = rot(proj[:, A_WIDTH:2 * A_WIDTH]).astype(BF16)
    v_ref[...] = proj[:, 2 * A_WIDTH:3 * A_WIDTH].astype(BF16)
    u = proj[:, 3 * A_WIDTH:]
    c = u.shape[1] // 2
    glu_ref[...] = u[:, :c] * jax.nn.sigmoid(u[:, c:])


def _hyb_in(h, g, w, pos, invf, shi, slo, *, tm):
    t, d = h.shape
    cc = (w.shape[1] - 3 * A_WIDTH) // 2
    row = lambda width: pl.BlockSpec((tm, width), lambda i: (i, 0))
    return pl.pallas_call(
        _hyb_in_body,
        grid=(t // tm,),
        in_specs=[row(d), _resident((1, d)), _resident(w.shape), row(1),
                  _resident((1, LANES)), _resident((1, LANES)), _resident((1, LANES))],
        out_specs=[row(A_WIDTH), row(A_WIDTH), row(A_WIDTH), row(cc)],
        out_shape=[jax.ShapeDtypeStruct((t, A_WIDTH), BF16)] * 3
        + [jax.ShapeDtypeStruct((t, cc), F32)],
        compiler_params=_params(1),
        name="hyb_in",
    )(h, g, w, pos, invf, shi, slo)


def _attn_body(q_ref, kc_ref, kp_ref, vc_ref, vp_ref, bias_ref, o_ref, lse_ref, kcat, vcat, *, tq):
    i = pl.program_id(2)
    w = WIN_BLOCK
    kcat[0:w, :] = kp_ref[0]
    kcat[w:, :] = kc_ref[0]
    vcat[0:w, :] = vp_ref[0]
    vcat[w:, :] = vc_ref[0]
    slab = MXU_DIM
    hps = slab // A_HEAD_DIM
    lane_head = lax.broadcasted_iota(jnp.int32, (1, slab), 1) // A_HEAD_DIM
    lane_idx = lax.broadcasted_iota(jnp.int32, (1, LANES), 1)
    for j in range(tq // w):
        first = jnp.where(jnp.logical_and(i == 0, j == 0), 1, 0) if j == 0 else 0
        bias = bias_ref[first]
        lse_tile = jnp.zeros((w, LANES), F32)
        for s in range(A_WIDTH // slab):
            cols = slice(s * slab, (s + 1) * slab)
            qs = q_ref[0, j * w:(j + 1) * w, cols]
            lhs = jnp.concatenate(
                [jnp.where(lane_head == hh, qs, jnp.zeros_like(qs)) for hh in range(hps)], axis=0)
            sc = _dot_nt(lhs, kcat[j * w:(j + 2) * w, cols]) + bias
            m = jnp.max(sc, axis=1, keepdims=True)
            p = jnp.exp(sc - m)
            den = jnp.sum(p, axis=1, keepdims=True)
            pv = _dot(p.astype(BF16), vcat[j * w:(j + 2) * w, cols]) / den
            lse = m + jnp.log(den)
            o_slab = jnp.zeros((w, slab), F32)
            for hh in range(hps):
                rows = slice(hh * w, (hh + 1) * w)
                o_slab = o_slab + jnp.where(lane_head == hh, pv[rows], 0.0)
                lse_tile = lse_tile + jnp.where(lane_idx == s * hps + hh, lse[rows], 0.0)
            o_ref[0, j * w:(j + 1) * w, cols] = o_slab
        lse_ref[0, j * w:(j + 1) * w, :] = lse_tile


def _attn_bias():
    w = WIN_BLOCK
    r = jnp.arange(w)[:, None]
    c = jnp.arange(2 * w)[None, :]
    band = (c >= r) & (c <= r + w)
    no_prev = band & (c >= w)
    bias = jnp.where(jnp.stack([band, no_prev]), 0.0, MASK_VALUE).astype(F32)
    return jnp.tile(bias, (1, MXU_DIM // A_HEAD_DIM, 1))


def _attn_branch(q, k, v, bias, dil):
    b, s, width = q.shape
    l = s // dil
    w = WIN_BLOCK
    tq = min(4 * w, l)
    view = lambda a: a.reshape(b, l, dil * width)
    cur = pl.BlockSpec((1, tq, width), lambda bi, c, i: (bi, i, c))
    prev = pl.BlockSpec((1, w, width), lambda bi, c, i: (bi, jnp.maximum(i * (tq // w) - 1, 0), c))
    o, lse = pl.pallas_call(
        functools.partial(_attn_body, tq=tq),
        grid=(b, dil, l // tq),
        in_specs=[cur, cur, prev, cur, prev, _resident(bias.shape)],
        out_specs=[cur, pl.BlockSpec((1, tq, LANES), lambda bi, c, i: (bi, i, c))],
        out_shape=[jax.ShapeDtypeStruct((b, l, dil * width), F32),
                   jax.ShapeDtypeStruct((b, l, dil * LANES), F32)],
        scratch_shapes=[pltpu.VMEM((tq + w, width), BF16), pltpu.VMEM((tq + w, width), BF16)],
        compiler_params=_params(3),
        name=f"attn_d{dil}",
    )(view(q), view(k), view(k), view(v), view(v), bias)
    return o.reshape(b, s, width), lse.reshape(b, s, LANES)


def _hyb_out_body(h_ref, o1_ref, o2_ref, o3_ref, l1_ref, l2_ref, l3_ref, glu_ref, gprev_ref,
                  dww_ref, dwb_ref, lng_ref, lnb_ref, wout_ref, out_ref, xpad, *, tm, halo,
                  tiles_per_seq):
    i = pl.program_id(0)
    l1, l2, l3 = l1_ref[...], l2_ref[...], l3_ref[...]
    mx = jnp.maximum(jnp.maximum(l1, l2), l3)
    e1, e2, e3 = jnp.exp(l1 - mx), jnp.exp(l2 - mx), jnp.exp(l3 - mx)
    tot = e1 + e2 + e3
    ws = (e1 / tot, e2 / tot, e3 / tot)
    lane = lax.broadcasted_iota(jnp.int32, (1, LANES), 1)
    heads_per_vreg = LANES // A_HEAD_DIM
    attn = []
    for s in range(A_WIDTH // LANES):
        cols = slice(s * LANES, (s + 1) * LANES)
        acc = None
        for wg, o_ref in zip(ws, (o1_ref, o2_ref, o3_ref)):
            wfull = wg[:, s * heads_per_vreg:s * heads_per_vreg + 1]
            for hh in range(1, heads_per_vreg):
                col = s * heads_per_vreg + hh
                wfull = jnp.where(lane < hh * A_HEAD_DIM, wfull, wg[:, col:col + 1])
            term = wfull * o_ref[:, cols]
            acc = term if acc is None else acc + term
        attn.append(acc)
    attn = jnp.concatenate(attn, axis=1)

    prev = gprev_ref[...]
    xpad[0:halo, :] = jnp.where(i % tiles_per_seq == 0, jnp.zeros_like(prev), prev)
    xpad[halo:, :] = glu_ref[...]
    conv = None
    for j in range(CONV_WIDTH):
        off = halo - (CONV_WIDTH - 1) + j
        term = xpad[off:off + tm, :] * dww_ref[j:j + 1, :]
        conv = term if conv is None else conv + term
    conv = conv + dwb_ref[...]
    mu = jnp.mean(conv, axis=-1, keepdims=True)
    xc = conv - mu
    y = xc * lax.rsqrt(jnp.mean(xc * xc, axis=-1, keepdims=True) + NORM_EPS)
    c = _silu(y * lng_ref[...] + lnb_ref[...])

    cat = jnp.concatenate([attn, c], axis=1).astype(BF16)
    out_ref[...] = h_ref[...] + _dot(cat, wout_ref[...])


def _hyb_out(h, os_, lses, glu, dw_w, dw_b, ln_g, ln_b, w_out, *, tm, seq):
    t, d = h.shape
    cc = glu.shape[1]
    halo = 4 * SUBLANES
    row = lambda width: pl.BlockSpec((tm, width), lambda i: (i, 0))
    prev = pl.BlockSpec((halo, cc), lambda i: (jnp.maximum(i * (tm // halo) - 1, 0), 0))
    body = functools.partial(_hyb_out_body, tm=tm, halo=halo, tiles_per_seq=seq // tm)
    return pl.pallas_call(
        body,
        grid=(t // tm,),
        in_specs=[row(d)] + [row(A_WIDTH)] * 3 + [row(LANES)] * 3 + [row(cc), prev,
                  _resident(dw_w.shape), _resident((1, cc)), _resident((1, cc)),
                  _resident((1, cc)), _resident(w_out.shape)],
        out_specs=row(d),
        out_shape=jax.ShapeDtypeStruct((t, d), F32),
        scratch_shapes=[pltpu.VMEM((tm + halo, cc), F32)],
        compiler_params=_params(1),
        name="hyb_out",
    )(h, *os_, *lses, glu, glu, dw_w, dw_b, ln_g, ln_b, w_out)


def _gdn_in_body(h_ref, g_ref, w_ref, ws_ref, cw_ref, alog_ref, dtb_ref,
                 qkv_ref, z_ref, bg_ref, xpad, *, tm, tiles_per_seq):
    i = pl.program_id(0)
    xn = _rms_bf16(h_ref[...], g_ref[...])
    proj = _dot(xn, w_ref[...])
    small = _dot(xn, ws_ref[...])
    hw = GDN_HEADS * GDN_DIM
    halo = SUBLANES

    @pl.when(i % tiles_per_seq == 0)
    def _():
        xpad[0:halo, :] = jnp.zeros((halo, 3 * hw), F32)

    @pl.when(i % tiles_per_seq != 0)
    def _():
        xpad[0:halo, :] = xpad[tm:tm + halo, :]

    xpad[halo:, :] = proj[:, :3 * hw]
    conv = None
    for j in range(GDN_SHORT_CONV):
        off = halo - (GDN_SHORT_CONV - 1) + j
        term = xpad[off:off + tm, :] * cw_ref[j:j + 1, :]
        conv = term if conv is None else conv + term
    qkv = _silu(conv)

    def l2n(x):
        return x * lax.rsqrt(jnp.sum(x * x, axis=-1, keepdims=True) + NORM_EPS)

    for hh in range(2 * GDN_HEADS):
        cols = slice(hh * GDN_DIM, (hh + 1) * GDN_DIM)
        qkv_ref[:, cols] = l2n(qkv[:, cols])
    qkv_ref[:, 2 * hw:] = qkv[:, 2 * hw:]
    z_ref[...] = proj[:, 3 * hw:]

    x = small + dtb_ref[...]
    softplus = jnp.maximum(x, 0.0) + jnp.log1p(jnp.exp(-jnp.abs(x)))
    lane = lax.broadcasted_iota(jnp.int32, (1, LANES), 1)
    bg_ref[...] = jnp.where(lane < GDN_HEADS, jax.nn.sigmoid(small),
                            -jnp.exp(alog_ref[...]) * softplus)


def _gdn_in(h, g, w_main, w_small, conv_w, alog, dtb, *, tm, seq):
    t, d = h.shape
    hw = GDN_HEADS * GDN_DIM
    row = lambda width: pl.BlockSpec((tm, width), lambda i: (i, 0))
    body = functools.partial(_gdn_in_body, tm=tm, tiles_per_seq=seq // tm)
    return pl.pallas_call(
        body,
        grid=(t // tm,),
        in_specs=[row(d), _resident((1, d)),
                  _resident(w_main.shape), _resident(w_small.shape), _resident(conv_w.shape),
                  _resident((1, LANES)), _resident((1, LANES))],
        out_specs=[row(3 * hw), row(hw), row(LANES)],
        out_shape=[jax.ShapeDtypeStruct((t, 3 * hw), F32), jax.ShapeDtypeStruct((t, hw), F32),
                   jax.ShapeDtypeStruct((t, LANES), F32)],
        scratch_shapes=[pltpu.VMEM((tm + SUBLANES, 3 * hw), F32)],
        compiler_params=_params(1),
        name="gdn_in",
    )(h, g, w_main, w_small, conv_w, alog, dtb)


def _split2(a):
    hi = a.astype(BF16)
    lo = (a - hi.astype(F32)).astype(BF16)
    return hi, lo


def _mm_f32(a, b):
    a_hi, a_lo = _split2(a)
    b_hi, b_lo = _split2(b)
    lhs = jnp.concatenate([a_hi, a_hi, a_lo, a_lo], axis=1)
    rhs = jnp.concatenate([b_hi, b_lo, b_hi, b_lo], axis=0)
    return _dot(lhs, rhs)


def _unit_lower_inverses(lms, eye, nilpotency):
    n = lms[0].shape[0]
    pows = [-lm for lm in lms]
    sums = [eye + p for p in pows]
    pows = [_mm_f32(p, p) for p in pows]
    m = 2
    while 2 * m < nilpotency:
        prods = [_mm_f32(jnp.concatenate([p, s], axis=0), p) for p, s in zip(pows, sums)]
        sums = [s + r[n:] for s, r in zip(sums, prods)]
        pows = [r[:n] for r in prods]
        m *= 2
    return [s + _mm_f32(s, p) for s, p in zip(sums, pows)]


def _gdn_core_body(qkv_ref, z_ref, bg_ref, ng_ref, o_ref, state, *, tc):
    n = pl.program_id(1)
    c = GDN_CHUNK
    pr = 2 * c
    hw = GDN_HEADS * GDN_DIM
    scale = GDN_DIM ** -0.5

    @pl.when(n == 0)
    def _():
        state[...] = jnp.zeros_like(state)

    ri = lax.broadcasted_iota(jnp.int32, (tc, tc), 0)
    ci = lax.broadcasted_iota(jnp.int32, (tc, tc), 1)
    tri_l = jnp.logical_and(ri // c == ci // c, ri >= ci).astype(BF16)
    bg = bg_ref[...]
    g1 = bg.astype(BF16)
    r1 = bg - g1.astype(F32)
    g2 = r1.astype(BF16)
    g3 = (r1 - g2.astype(F32)).astype(BF16)
    gc = _dot(jnp.concatenate([tri_l] * 3, axis=1), jnp.concatenate([g1, g2, g3], axis=0))
    gct = gc.T

    pi = lax.broadcasted_iota(jnp.int32, (pr, pr), 0)
    pj = lax.broadcasted_iota(jnp.int32, (pr, pr), 1)
    same_chunk = pi // c == pj // c
    eye_f = (pi == pj).astype(F32)
    causal = jnp.logical_and(same_chunk, pi >= pj)
    strict = jnp.logical_and(same_chunk, pi > pj)
    row_chunk = lax.broadcasted_iota(jnp.int32, (pr, 1), 0) // c

    units = [(hh, pp) for pp in range(tc // pr) for hh in range(GDN_HEADS)]

    lms, attns, rhs2s, q_decs, kd_ts = [], [], [], [], []
    for hh, pp in units:
        rows = slice(pp * pr, (pp + 1) * pr)
        cols = slice(hh * GDN_DIM, (hh + 1) * GDN_DIM)
        q = qkv_ref[rows, cols] * scale
        k = qkv_ref[rows, hw + hh * GDN_DIM:hw + (hh + 1) * GDN_DIM]
        v = qkv_ref[rows, 2 * hw + hh * GDN_DIM:2 * hw + (hh + 1) * GDN_DIM]
        b_col = bg[rows, hh:hh + 1]
        g_col = gc[rows, GDN_HEADS + hh:GDN_HEADS + hh + 1]
        g_row = gct[GDN_HEADS + hh:GDN_HEADS + hh + 1, rows]
        decay = jnp.exp(jnp.where(causal, g_col - g_row, MASK_VALUE))
        eg = jnp.exp(g_col)
        kb = k * b_col
        p1 = _dot_nt(jnp.concatenate([kb, q], axis=0).astype(BF16), k.astype(BF16))
        lms.append(jnp.where(strict, p1[:pr] * decay, 0.0))
        attns.append((p1[pr:] * decay).astype(BF16))
        rhs2s.append(jnp.concatenate([v * b_col, kb * eg], axis=1).astype(BF16))
        q_decs.append((q * eg).astype(BF16))
        kds = []
        for cc in range(2):
            last = pp * pr + cc * c + c - 1
            g_last = gc[last:last + 1, GDN_HEADS + hh:GDN_HEADS + hh + 1]
            kd = jnp.where(row_chunk == cc, k * jnp.exp(g_last - g_col), 0.0)
            kds.append((kd.T.astype(BF16), jnp.exp(g_last)))
        kd_ts.append(kds)

    t_invs = _unit_lower_inverses(lms, eye_f, c)
    p2s = [_dot(t.astype(BF16), r) for t, r in zip(t_invs, rhs2s)]

    for pp in range(tc // pr):
        us = {hh: p2s[pp * GDN_HEADS + hh] for hh in range(GDN_HEADS)}
        v_done = {hh: [] for hh in range(GDN_HEADS)}
        for cc in range(2):
            cr = slice(cc * c, (cc + 1) * c)
            orow = slice(pp * pr + cc * c, pp * pr + (cc + 1) * c)
            s_olds = [state[hh] for hh in range(GDN_HEADS)]
            p3s = []
            for hh in range(GDN_HEADS):
                unit = pp * GDN_HEADS + hh
                w = us[hh][cr, GDN_DIM:]
                lhs = jnp.concatenate([w.astype(BF16), q_decs[unit][cr]], axis=0)
                p3s.append(_dot(lhs, s_olds[hh].astype(BF16)))
            for hh in range(GDN_HEADS):
                unit = pp * GDN_HEADS + hh
                u = us[hh][:, :GDN_DIM]
                v_new = u[cr] - p3s[hh][:c]
                v_done[hh].append(v_new)
                v_pair = jnp.concatenate(v_done[hh] + [u[c:]] * (1 - cc), axis=0).astype(BF16)
                kd_t, decay_last = kd_ts[unit][cc]
                p45 = _dot(jnp.concatenate([attns[unit][cr], kd_t], axis=0), v_pair)
                state[hh] = s_olds[hh] * decay_last + p45[c:]
                o = p3s[hh][c:] + p45[:c]
                on = (o * lax.rsqrt(jnp.mean(o * o, axis=-1, keepdims=True) + NORM_EPS)
                      * ng_ref[...])
                cols = slice(hh * GDN_DIM, (hh + 1) * GDN_DIM)
                o_ref[orow, cols] = (on * _silu(z_ref[orow, cols])).astype(BF16)


def _gdn_core(qkv, z, bg, norm_g, *, batch, seq, tc):
    t, hw = z.shape
    tiles = seq // tc
    blk = lambda width: pl.BlockSpec((tc, width), lambda b, n: (b * tiles + n, 0))
    return pl.pallas_call(
        functools.partial(_gdn_core_body, tc=tc),
        grid=(batch, tiles),
        in_specs=[blk(3 * hw), blk(hw), blk(LANES), _resident((1, GDN_DIM))],
        out_specs=blk(hw),
        out_shape=jax.ShapeDtypeStruct((t, hw), BF16),
        scratch_shapes=[pltpu.VMEM((GDN_HEADS, GDN_DIM, GDN_DIM), F32)],
        compiler_params=_params(2),
        name="gdn_core",
    )(qkv, z, bg, norm_g)


def _gdn_out_body(h_ref, o_ref, w_ref, out_ref):
    out_ref[...] = h_ref[...] + _dot(o_ref[...], w_ref[...])


def _gdn_out(h, o, w_out, *, tm):
    t, d = h.shape
    row = lambda width: pl.BlockSpec((tm, width), lambda i: (i, 0))
    return pl.pallas_call(
        _gdn_out_body,
        grid=(t // tm,),
        in_specs=[row(d), row(o.shape[1]), _resident(w_out.shape)],
        out_specs=row(d),
        out_shape=jax.ShapeDtypeStruct((t, d), F32),
        compiler_params=_params(1),
        name="gdn_out",
    )(h, o, w_out)


def _tile(n, want):
    t = min(want, n)
    assert n % t == 0, (n, t)
    return t


def kernel(x, positions, ffn1_norm, ffn1_w_in, ffn1_w_out, mix_norm, ffn2_norm, ffn2_w_in,
           ffn2_w_out, hyb_w_in, hyb_dw_w, hyb_dw_b, hyb_ln_g, hyb_ln_b, hyb_w_out,
           gdn_w_in, gdn_conv_w, gdn_A_log, gdn_dt_bias, gdn_norm_g, gdn_w_out, final_norm):
    batch, seq, d = x.shape
    depth = ffn1_norm.shape[0]
    t = batch * seq
    h = x.reshape(t, d)
    pos = positions.reshape(t, 1)
    row = lambda a: a.reshape(1, -1).astype(F32)

    lane = jnp.arange(LANES) % A_HEAD_DIM
    inv_freq = jnp.power(jnp.float32(ROPE_THETA),
                         -jnp.arange(0, ROT_DIM, 2, dtype=F32) / ROT_DIM)
    half = ROT_DIM // 2
    invf = jnp.where(lane < ROT_DIM, inv_freq[lane % half], 0.0).astype(F32).reshape(1, LANES)
    shi = jnp.where((lane >= half) & (lane < ROT_DIM), 1.0, 0.0).astype(F32).reshape(1, LANES)
    slo = jnp.where(lane < half, -1.0, 0.0).astype(F32).reshape(1, LANES)
    bias = _attn_bias()

    gdn_hw = GDN_HEADS * GDN_DIM
    tm_ffn = _tile(t, 1024)
    tm_mix = _tile(seq, 512)
    final_g = row(final_norm)

    for layer in range(depth):
        i = layer // 2
        h = _ffn(h, row(ffn1_norm[layer]), ffn1_w_in[layer].astype(BF16),
                 ffn1_w_out[layer].astype(BF16), final_g, final_norm=False, tm=tm_ffn)
        if layer % 2 == 0:
            q, k, v, glu = _hyb_in(h, row(mix_norm[layer]), hyb_w_in[i].astype(BF16), pos,
                                   invf, shi, slo, tm=tm_mix)
            shape3 = (batch, seq, A_WIDTH)
            outs, lses = [], []
            for dil in DILATIONS:
                o_g, lse_g = _attn_branch(q.reshape(shape3), k.reshape(shape3),
                                          v.reshape(shape3), bias, dil)
                outs.append(o_g.reshape(t, A_WIDTH))
                lses.append(lse_g.reshape(t, LANES))
            h = _hyb_out(h, outs, lses, glu, hyb_dw_w[i], row(hyb_dw_b[i]), row(hyb_ln_g[i]),
                         row(hyb_ln_b[i]), hyb_w_out[i].astype(BF16), tm=tm_mix, seq=seq)
        else:
            w_in = gdn_w_in[i]
            w_main = w_in[:, :4 * gdn_hw].astype(BF16)
            w_small = jnp.pad(w_in[:, 4 * gdn_hw:],
                              ((0, 0), (0, LANES - 2 * GDN_HEADS))).astype(BF16)
            lane_pad = lambda a: jnp.pad(a.astype(F32), (GDN_HEADS, LANES - 2 * GDN_HEADS)
                                         ).reshape(1, LANES)
            qkv, z, bg = _gdn_in(h, row(mix_norm[layer]), w_main, w_small, gdn_conv_w[i],
                                 lane_pad(gdn_A_log[i]), lane_pad(gdn_dt_bias[i]),
                                 tm=tm_mix, seq=seq)
            o = _gdn_core(qkv, z, bg, row(gdn_norm_g[i]), batch=batch, seq=seq,
                          tc=_tile(seq, 2 * GDN_CHUNK))
            h = _gdn_out(h, o, gdn_w_out[i].astype(BF16), tm=tm_ffn)
        h = _ffn(h, row(ffn2_norm[layer]), ffn2_w_in[layer].astype(BF16),
                 ffn2_w_out[layer].astype(BF16), final_g,
                 final_norm=(layer == depth - 1), tm=tm_ffn)
    return h.reshape(batch, seq, d)
```

```python
import functools

import jax
import jax.numpy as jnp
from jax import lax
from jax.experimental import pallas as pl
from jax.experimental.pallas import tpu as pltpu

F32 = jnp.float32
BF16 = jnp.bfloat16

NORM_EPS = 1e-6
ROPE_THETA = 500000.0
A_HEADS = 8
A_HEAD_DIM = 64
A_WIDTH = A_HEADS * A_HEAD_DIM
ROT_DIM = A_HEAD_DIM // 4
DILATIONS = (1, 4, 16)
WIN_BLOCK = 128
CONV_WIDTH = 31
GDN_HEADS = 8
GDN_DIM = 128
GDN_CHUNK = 64
GDN_SHORT_CONV = 4

LANES = 128
SUBLANES = 8
MXU_DIM = 256
VMEM_LIMIT = 56 * 1024 * 1024
MASK_VALUE = -1e30


def _params(n_axes):
    return pltpu.CompilerParams(dimension_semantics=("arbitrary",) * n_axes,
                                vmem_limit_bytes=VMEM_LIMIT)


def _resident(shape):
    nd = len(shape)
    return pl.BlockSpec(shape, lambda *_: (0,) * nd, pipeline_mode=pl.Buffered(1))


def _rms_bf16(x, g):
    ms = jnp.mean(x * x, axis=-1, keepdims=True)
    return (x * lax.rsqrt(ms + NORM_EPS) * g).astype(BF16)


def _silu(x):
    return x * jax.nn.sigmoid(x)


def _dot(a, b):
    return jnp.dot(a, b, preferred_element_type=F32)


def _dot_nt(a, b):
    return lax.dot_general(a, b, (((1,), (1,)), ((), ())), preferred_element_type=F32)


def _dot_tn(a, b):
    return lax.dot_general(a, b, (((0,), (0,)), ((), ())), preferred_element_type=F32)


def _ffn_body(h_ref, g_ref, win_ref, wout_ref, fg_ref, o_ref, xn_ref, acc_ref, *, ffn, chunks,
              final_norm):
    xn_ref[...] = _rms_bf16(h_ref[...], g_ref[...])
    for n, (c0, cw) in enumerate(chunks):
        xn = xn_ref[...]
        gate = _dot(xn, win_ref[:, c0:c0 + cw])
        up = _dot(xn, win_ref[:, ffn + c0:ffn + c0 + cw])
        part = _dot((_silu(gate) * up).astype(BF16), wout_ref[c0:c0 + cw, :])
        if n == 0:
            acc_ref[...] = part
        else:
            acc_ref[...] += part
    out = h_ref[...] + 0.5 * acc_ref[...]
    if final_norm:
        ms = jnp.mean(out * out, axis=-1, keepdims=True)
        out = out * lax.rsqrt(ms + NORM_EPS) * fg_ref[...]
    o_ref[...] = out


def _ffn_chunks(ffn, width):
    out, c0 = [], 0
    while c0 < ffn:
        cw = min(width, ffn - c0)
        out.append((c0, cw))
        c0 += cw
    return tuple(out)


def _ffn(h, g, w_in, w_out, final_g, *, final_norm, tm):
    t, d = h.shape
    ffn = w_out.shape[0]
    body = functools.partial(_ffn_body, ffn=ffn, chunks=_ffn_chunks(ffn, 2 * MXU_DIM),
                             final_norm=final_norm)
    return pl.pallas_call(
        body,
        grid=(t // tm,),
        in_specs=[pl.BlockSpec((tm, d), lambda i: (i, 0)),
                  _resident((1, d)), _resident(w_in.shape), _resident(w_out.shape),
                  _resident((1, d))],
        out_specs=pl.BlockSpec((tm, d), lambda i: (i, 0)),
        out_shape=jax.ShapeDtypeStruct((t, d), F32),
        scratch_shapes=[pltpu.VMEM((tm, d), BF16), pltpu.VMEM((tm, d), F32)],
        compiler_params=_params(1),
        name="ffn_final" if final_norm else "ffn",
    )(h, g, w_in, w_out, final_g)


def _hyb_in_body(h_ref, g_ref, w_ref, pos_ref, invf_ref, shi_ref, slo_ref, *refs, tm):
    nd = len(DILATIONS)
    qkv_refs = [refs[3 * n:3 * n + 3] for n in range(nd)]
    glu_ref, stage = refs[3 * nd], refs[3 * nd + 1]
    xn = _rms_bf16(h_ref[...], g_ref[...])
    proj = _dot(xn, w_ref[...])
    ang = pos_ref[...].astype(F32) * invf_ref[...]
    cos, sin = jnp.cos(ang), jnp.sin(ang)
    reps = A_WIDTH // LANES
    cos4 = jnp.concatenate([cos] * reps, axis=1)
    sin_hi = jnp.concatenate([sin * shi_ref[...]] * reps, axis=1)
    sin_lo = jnp.concatenate([sin * slo_ref[...]] * reps, axis=1)
    half = ROT_DIM // 2

    def rot(x):
        return (x * cos4 + pltpu.roll(x, half, 1) * sin_hi
                + pltpu.roll(x, A_WIDTH - half, 1) * sin_lo)

    qkv = (rot(proj[:, :A_WIDTH]) * (A_HEAD_DIM ** -0.5),
           rot(proj[:, A_WIDTH:2 * A_WIDTH]),
           proj[:, 2 * A_WIDTH:3 * A_WIDTH])
    u = proj[:, 3 * A_WIDTH:]
    c = u.shape[1] // 2
    glu_ref[...] = u[:, :c] * jax.nn.sigmoid(u[:, c:])

    slabs = A_WIDTH // LANES
    for a, x in enumerate(qkv):
        for s in range(slabs):
            stage[a * slabs + s] = x[:, s * LANES:(s + 1) * LANES]
    for dil, out_refs in zip(DILATIONS, qkv_refs):
        for a, out_ref in enumerate(out_refs):
            if dil == 1:
                out_ref[...] = qkv[a].astype(BF16)
                continue
            for cls in range(dil):
                for s in range(slabs):
                    piece = stage[a * slabs + s, pl.ds(cls, tm // dil, stride=dil), :]
                    lane0 = cls * A_WIDTH + s * LANES
                    out_ref[:, lane0:lane0 + LANES] = piece.astype(BF16)


def _hyb_in(h, g, w, pos, invf, shi, slo, *, tm):
    t, d = h.shape
    cc = (w.shape[1] - 3 * A_WIDTH) // 2
    row = lambda width: pl.BlockSpec((tm, width), lambda i: (i, 0))
    dil_specs, dil_shapes = [], []
    for dil in DILATIONS:
        dil_specs += [pl.BlockSpec((tm // dil, dil * A_WIDTH), lambda i: (i, 0))] * 3
        dil_shapes += [jax.ShapeDtypeStruct((t // dil, dil * A_WIDTH), BF16)] * 3
    outs = pl.pallas_call(
        functools.partial(_hyb_in_body, tm=tm),
        grid=(t // tm,),
        in_specs=[row(d), _resident((1, d)), _resident(w.shape), row(1),
                  _resident((1, LANES)), _resident((1, LANES)), _resident((1, LANES))],
        out_specs=dil_specs + [row(cc)],
        out_shape=dil_shapes + [jax.ShapeDtypeStruct((t, cc), F32)],
        scratch_shapes=[pltpu.VMEM((3 * A_WIDTH // LANES, tm, LANES), F32)],
        compiler_params=_params(1),
        name="hyb_in",
    )(h, g, w, pos, invf, shi, slo)
    n = len(DILATIONS)
    return [tuple(outs[3 * i:3 * i + 3]) for i in range(n)], outs[3 * n]


def _attn_body(q_ref, kc_ref, kp_ref, vc_ref, vp_ref, bias_ref, o_ref, lse_ref, kcat, vcat, *, tq):
    i = pl.program_id(2)
    w = WIN_BLOCK
    kcat[0:w, :] = kp_ref[0]
    kcat[w:, :] = kc_ref[0]
    vcat[0:w, :] = vp_ref[0]
    vcat[w:, :] = vc_ref[0]
    slab = MXU_DIM
    hps = slab // A_HEAD_DIM
    lane_head = lax.broadcasted_iota(jnp.int32, (1, slab), 1) // A_HEAD_DIM
    lane_idx = lax.broadcasted_iota(jnp.int32, (1, LANES), 1)
    for j in range(tq // w):
        first = jnp.where(jnp.logical_and(i == 0, j == 0), 1, 0) if j == 0 else 0
        bias = bias_ref[first]
        lse_tile = jnp.zeros((w, LANES), F32)
        for s in range(A_WIDTH // slab):
            cols = slice(s * slab, (s + 1) * slab)
            qs = q_ref[0, j * w:(j + 1) * w, cols]
            lhs = jnp.concatenate(
                [jnp.where(lane_head == hh, qs, jnp.zeros_like(qs)) for hh in range(hps)], axis=0)
            sc = _dot_nt(lhs, kcat[j * w:(j + 2) * w, cols]) + bias
            m = jnp.max(sc, axis=1, keepdims=True)
            p = jnp.exp(sc - m)
            den = jnp.sum(p, axis=1, keepdims=True)
            pv = _dot(p.astype(BF16), vcat[j * w:(j + 2) * w, cols]) / den
            lse = m + jnp.log(den)
            o_slab = jnp.zeros((w, slab), F32)
            for hh in range(hps):
                rows = slice(hh * w, (hh + 1) * w)
                o_slab = o_slab + jnp.where(lane_head == hh, pv[rows], 0.0)
                lse_tile = lse_tile + jnp.where(lane_idx == s * hps + hh, lse[rows], 0.0)
            o_ref[0, j * w:(j + 1) * w, cols] = o_slab
        lse_ref[0, j * w:(j + 1) * w, :] = lse_tile


def _attn_bias():
    w = WIN_BLOCK
    r = jnp.arange(w)[:, None]
    c = jnp.arange(2 * w)[None, :]
    band = (c >= r) & (c <= r + w)
    no_prev = band & (c >= w)
    bias = jnp.where(jnp.stack([band, no_prev]), 0.0, MASK_VALUE).astype(F32)
    return jnp.tile(bias, (1, MXU_DIM // A_HEAD_DIM, 1))


def _attn_branch(q, k, v, bias, dil, batch):
    width = A_WIDTH
    b = batch
    l = q.shape[0] // b
    w = WIN_BLOCK
    tq = min(4 * w, l)
    view = lambda a: a.reshape(b, l, dil * width)
    cur = pl.BlockSpec((1, tq, width), lambda bi, c, i: (bi, i, c))
    prev = pl.BlockSpec((1, w, width), lambda bi, c, i: (bi, jnp.maximum(i * (tq // w) - 1, 0), c))
    o, lse = pl.pallas_call(
        functools.partial(_attn_body, tq=tq),
        grid=(b, dil, l // tq),
        in_specs=[cur, cur, prev, cur, prev, _resident(bias.shape)],
        out_specs=[cur, pl.BlockSpec((1, tq, LANES), lambda bi, c, i: (bi, i, c))],
        out_shape=[jax.ShapeDtypeStruct((b, l, dil * width), F32),
                   jax.ShapeDtypeStruct((b, l, dil * LANES), F32)],
        scratch_shapes=[pltpu.VMEM((tq + w, width), BF16), pltpu.VMEM((tq + w, width), BF16)],
        compiler_params=_params(3),
        name=f"attn_d{dil}",
    )(view(q), view(k), view(k), view(v), view(v), bias)
    return o.reshape(b * l, dil * width), lse.reshape(b * l, dil * LANES)


def _conv_block(xpad, dww_ref, r0, nrows, lanes, halo):
    base = r0 + halo - SUBLANES
    acc = None
    for b in range(SUBLANES):
        z = None
        for a in range(-(-CONV_WIDTH // SUBLANES)):
            delay = SUBLANES * a + b
            if delay >= CONV_WIDTH:
                continue
            j = CONV_WIDTH - 1 - delay
            start = base - SUBLANES * a
            term = xpad[start:start + nrows + SUBLANES, lanes] * dww_ref[j:j + 1, lanes]
            z = term if z is None else z + term
        zs = z if b == 0 else pltpu.roll(z, b, 0)
        acc = zs if acc is None else acc + zs
    return acc[SUBLANES:]


def _hyb_out_body(h_ref, o1_ref, o2_ref, o3_ref, l1_ref, l2_ref, l3_ref, glu_ref, gprev_ref,
                  dww_ref, dwb_ref, lng_ref, lnb_ref, wout_ref, out_ref,
                  xpad, ostage, lstage, cat, *, tm, halo, tiles_per_seq, rb, sub):
    i = pl.program_id(0)
    slabs = A_WIDTH // LANES
    for n, (dil, o_ref, l_ref) in enumerate(zip(DILATIONS[1:], (o2_ref, o3_ref), (l2_ref, l3_ref))):
        for cls in range(dil):
            rows = pl.ds(cls, tm // dil, stride=dil)
            for s in range(slabs):
                lane0 = cls * A_WIDTH + s * LANES
                ostage[n * slabs + s, rows, :] = o_ref[:, lane0:lane0 + LANES]
            lstage[n, rows, :] = l_ref[:, cls * LANES:(cls + 1) * LANES]

    prev = gprev_ref[...]
    xpad[0:halo, :] = jnp.where(i % tiles_per_seq == 0, jnp.zeros_like(prev), prev)
    xpad[halo:, :] = glu_ref[...]

    lane = lax.broadcasted_iota(jnp.int32, (1, LANES), 1)
    heads_per_vreg = LANES // A_HEAD_DIM
    for blk in range(tm // rb):
        rows = slice(blk * rb, (blk + 1) * rb)
        l1, l2, l3 = l1_ref[rows, :], lstage[0, rows, :], lstage[1, rows, :]
        mx = jnp.maximum(jnp.maximum(l1, l2), l3)
        e1, e2, e3 = jnp.exp(l1 - mx), jnp.exp(l2 - mx), jnp.exp(l3 - mx)
        tot = e1 + e2 + e3
        ws = (e1 / tot, e2 / tot, e3 / tot)
        pieces = []
        for s in range(slabs):
            o_gs = (o1_ref[rows, s * LANES:(s + 1) * LANES], ostage[s, rows, :],
                    ostage[slabs + s, rows, :])
            acc = None
            for wg, o_g in zip(ws, o_gs):
                wfull = wg[:, s * heads_per_vreg:s * heads_per_vreg + 1]
                for hh in range(1, heads_per_vreg):
                    col = s * heads_per_vreg + hh
                    wfull = jnp.where(lane < hh * A_HEAD_DIM, wfull, wg[:, col:col + 1])
                term = wfull * o_g
                acc = term if acc is None else acc + term
            pieces.append(acc)

        cw = glu_ref.shape[1]
        conv = jnp.concatenate(
            [_conv_block(xpad, dww_ref, blk * rb, rb, slice(s * LANES, (s + 1) * LANES), halo)
             for s in range(cw // LANES)], axis=1) + dwb_ref[...]
        mu = jnp.mean(conv, axis=-1, keepdims=True)
        xc = conv - mu
        y = xc * lax.rsqrt(jnp.mean(xc * xc, axis=-1, keepdims=True) + NORM_EPS)
        c = _silu(y * lng_ref[...] + lnb_ref[...])
        cat[rows, :] = jnp.concatenate(pieces + [c], axis=1).astype(BF16)

        if (blk + 1) % (sub // rb) == 0:
            srows = slice((blk + 1) * rb - sub, (blk + 1) * rb)
            out_ref[srows, :] = h_ref[srows, :] + _dot(cat[srows, :], wout_ref[...])


def _hyb_out(h, os_, lses, glu, dw_w, dw_b, ln_g, ln_b, w_out, *, tm, seq):
    t, d = h.shape
    cc = glu.shape[1]
    halo = 4 * SUBLANES
    assert DILATIONS[0] == 1 and halo >= SUBLANES * -(-CONV_WIDTH // SUBLANES)
    row = lambda width: pl.BlockSpec((tm, width), lambda i: (i, 0))
    dil_row = lambda width: [pl.BlockSpec((tm // dil, dil * width), lambda i: (i, 0))
                             for dil in DILATIONS]
    prev = pl.BlockSpec((halo, cc), lambda i: (jnp.maximum(i * (tm // halo) - 1, 0), 0))
    body = functools.partial(_hyb_out_body, tm=tm, halo=halo, tiles_per_seq=seq // tm,
                             rb=4 * SUBLANES, sub=min(tm, MXU_DIM))
    nd = len(DILATIONS) - 1
    return pl.pallas_call(
        body,
        grid=(t // tm,),
        in_specs=[row(d)] + dil_row(A_WIDTH) + dil_row(LANES) + [row(cc), prev,
                  _resident(dw_w.shape), _resident((1, cc)), _resident((1, cc)),
                  _resident((1, cc)), _resident(w_out.shape)],
        out_specs=row(d),
        out_shape=jax.ShapeDtypeStruct((t, d), F32),
        scratch_shapes=[pltpu.VMEM((tm + halo, cc), F32),
                        pltpu.VMEM((nd * A_WIDTH // LANES, tm, LANES), F32),
                        pltpu.VMEM((nd, tm, LANES), F32),
                        pltpu.VMEM((tm, d), BF16)],
        compiler_params=_params(1),
        name="hyb_out",
    )(h, *os_, *lses, glu, glu, dw_w, dw_b, ln_g, ln_b, w_out)


def _gdn_in_body(h_ref, g_ref, w_ref, ws_ref, cw_ref, alog_ref, dtb_ref,
                 qkv_ref, z_ref, bg_ref, xn_ref, carry, *, tm, tiles_per_seq, chunk):
    i = pl.program_id(0)
    xn_ref[...] = _rms_bf16(h_ref[...], g_ref[...])
    hw = GDN_HEADS * GDN_DIM
    halo = SUBLANES
    taps = GDN_SHORT_CONV

    @pl.when(i % tiles_per_seq == 0)
    def _():
        carry[...] = jnp.zeros_like(carry)

    def l2n(x):
        return x * lax.rsqrt(jnp.sum(x * x, axis=-1, keepdims=True) + NORM_EPS)

    for c0 in range(0, 3 * hw, chunk):
        cols = slice(c0, c0 + chunk)
        pc = _dot(xn_ref[...], w_ref[:, cols])
        ext = jnp.concatenate([carry[:, cols], pc], axis=0)
        carry[:, cols] = pc[tm - halo:, :]
        x1 = pltpu.roll(ext, 1, 0)
        w0, w1, w2, w3 = (cw_ref[j:j + 1, cols] for j in range(taps))
        even = ext * w3 + x1 * w2
        odd = ext * w1 + x1 * w0
        qkv = _silu((even + pltpu.roll(odd, 2, 0))[halo:])
        for hh in range(chunk // GDN_DIM):
            hc = slice(hh * GDN_DIM, (hh + 1) * GDN_DIM)
            oc = slice(c0 + hh * GDN_DIM, c0 + (hh + 1) * GDN_DIM)
            qkv_ref[:, oc] = l2n(qkv[:, hc]) if c0 < 2 * hw else qkv[:, hc]
    for c0 in range(0, hw, chunk):
        z_ref[:, c0:c0 + chunk] = _dot(xn_ref[...], w_ref[:, 3 * hw + c0:3 * hw + c0 + chunk])

    small = _dot(xn_ref[...], ws_ref[...])
    x = small + dtb_ref[...]
    softplus = jnp.maximum(x, 0.0) + jnp.log1p(jnp.exp(-jnp.abs(x)))
    lane = lax.broadcasted_iota(jnp.int32, (1, LANES), 1)
    bg_ref[...] = jnp.where(lane < GDN_HEADS, jax.nn.sigmoid(small),
                            -jnp.exp(alog_ref[...]) * softplus)


def _gdn_in(h, g, w_main, w_small, conv_w, alog, dtb, *, tm, seq):
    t, d = h.shape
    hw = GDN_HEADS * GDN_DIM
    row = lambda width: pl.BlockSpec((tm, width), lambda i: (i, 0))
    body = functools.partial(_gdn_in_body, tm=tm, tiles_per_seq=seq // tm, chunk=2 * MXU_DIM)
    return pl.pallas_call(
        body,
        grid=(t // tm,),
        in_specs=[row(d), _resident((1, d)),
                  _resident(w_main.shape), _resident(w_small.shape), _resident(conv_w.shape),
                  _resident((1, LANES)), _resident((1, LANES))],
        out_specs=[row(3 * hw), row(hw), row(LANES)],
        out_shape=[jax.ShapeDtypeStruct((t, 3 * hw), F32), jax.ShapeDtypeStruct((t, hw), F32),
                   jax.ShapeDtypeStruct((t, LANES), F32)],
        scratch_shapes=[pltpu.VMEM((tm, d), BF16), pltpu.VMEM((SUBLANES, 3 * hw), F32)],
        compiler_params=_params(1),
        name="gdn_in",
    )(h, g, w_main, w_small, conv_w, alog, dtb)


def _split2(a):
    hi = a.astype(BF16)
    lo = (a - hi.astype(F32)).astype(BF16)
    return hi, lo


def _mm_f32(a, b):
    a_hi, a_lo = _split2(a)
    b_hi, b_lo = _split2(b)
    lhs = jnp.concatenate([a_hi, a_hi, a_lo], axis=1)
    rhs = jnp.concatenate([b_hi, b_lo, b_hi], axis=0)
    return _dot(lhs, rhs)


def _unit_lower_inverses(lms, eye, nilpotency):
    n = lms[0].shape[0]
    pows = [-lm for lm in lms]
    sums = [eye + p for p in pows]
    pows = [_mm_f32(p, p) for p in pows]
    m = 2
    while 2 * m < nilpotency:
        prods = [_mm_f32(jnp.concatenate([p, s], axis=0), p) for p, s in zip(pows, sums)]
        sums = [s + r[n:] for s, r in zip(sums, prods)]
        pows = [r[:n] for r in prods]
        m *= 2
    return [s + _mm_f32(s, p) for s, p in zip(sums, pows)]


def _gdn_core_body(qkv_ref, z_ref, bg_ref, ng_ref, o_ref, state, *, tc):
    n = pl.program_id(1)
    c = GDN_CHUNK
    pr = 2 * c
    hw = GDN_HEADS * GDN_DIM
    scale = GDN_DIM ** -0.5

    @pl.when(n == 0)
    def _():
        state[...] = jnp.zeros_like(state)

    ri = lax.broadcasted_iota(jnp.int32, (tc, tc), 0)
    ci = lax.broadcasted_iota(jnp.int32, (tc, tc), 1)
    tri_l = jnp.logical_and(ri // c == ci // c, ri >= ci).astype(BF16)
    bg = bg_ref[...]
    g1 = bg.astype(BF16)
    r1 = bg - g1.astype(F32)
    g2 = r1.astype(BF16)
    g3 = (r1 - g2.astype(F32)).astype(BF16)
    gc = _dot(jnp.concatenate([tri_l] * 3, axis=1), jnp.concatenate([g1, g2, g3], axis=0))
    gct = gc.T

    pi = lax.broadcasted_iota(jnp.int32, (pr, pr), 0)
    pj = lax.broadcasted_iota(jnp.int32, (pr, pr), 1)
    same_chunk = pi // c == pj // c
    eye_f = (pi == pj).astype(F32)
    causal = jnp.logical_and(same_chunk, pi >= pj)
    strict = jnp.logical_and(same_chunk, pi > pj)
    row_chunk = lax.broadcasted_iota(jnp.int32, (pr, 1), 0) // c

    units = [(hh, pp) for pp in range(tc // pr) for hh in range(GDN_HEADS)]

    lms, attns, rhs2s, q_decs, kd_ts = [], [], [], [], []
    for hh, pp in units:
        rows = slice(pp * pr, (pp + 1) * pr)
        cols = slice(hh * GDN_DIM, (hh + 1) * GDN_DIM)
        q = qkv_ref[rows, cols] * scale
        k = qkv_ref[rows, hw + hh * GDN_DIM:hw + (hh + 1) * GDN_DIM]
        v = qkv_ref[rows, 2 * hw + hh * GDN_DIM:2 * hw + (hh + 1) * GDN_DIM]
        b_col = bg[rows, hh:hh + 1]
        g_col = gc[rows, GDN_HEADS + hh:GDN_HEADS + hh + 1]
        g_row = gct[GDN_HEADS + hh:GDN_HEADS + hh + 1, rows]
        decay = jnp.exp(jnp.where(causal, g_col - g_row, MASK_VALUE))
        eg = jnp.exp(g_col)
        kb = k * b_col
        p1 = _dot_nt(jnp.concatenate([kb, q], axis=0).astype(BF16), k.astype(BF16))
        lms.append(jnp.where(strict, p1[:pr] * decay, 0.0))
        attns.append((p1[pr:] * decay).astype(BF16))
        rhs2s.append(jnp.concatenate([v * b_col, kb * eg], axis=1).astype(BF16))
        q_decs.append((q * eg).astype(BF16))
        kds = []
        for cc in range(2):
            last = pp * pr + cc * c + c - 1
            g_last = gc[last:last + 1, GDN_HEADS + hh:GDN_HEADS + hh + 1]
            kd = jnp.where(row_chunk == cc, k * jnp.exp(g_last - g_col), 0.0)
            kds.append((kd.T.astype(BF16), jnp.exp(g_last)))
        kd_ts.append(kds)

    t_invs = _unit_lower_inverses(lms, eye_f, c)
    p2s = [_dot(t.astype(BF16), r) for t, r in zip(t_invs, rhs2s)]

    for pp in range(tc // pr):
        us = {hh: p2s[pp * GDN_HEADS + hh] for hh in range(GDN_HEADS)}
        v_done = {hh: [] for hh in range(GDN_HEADS)}
        for cc in range(2):
            cr = slice(cc * c, (cc + 1) * c)
            orow = slice(pp * pr + cc * c, pp * pr + (cc + 1) * c)
            s_olds = [state[hh] for hh in range(GDN_HEADS)]
            p3s = []
            for hh in range(GDN_HEADS):
                unit = pp * GDN_HEADS + hh
                w = us[hh][cr, GDN_DIM:]
                lhs = jnp.concatenate([w.astype(BF16), q_decs[unit][cr]], axis=0)
                p3s.append(_dot(lhs, s_olds[hh].astype(BF16)))
            for hh in range(GDN_HEADS):
                unit = pp * GDN_HEADS + hh
                u = us[hh][:, :GDN_DIM]
                v_new = u[cr] - p3s[hh][:c]
                v_done[hh].append(v_new)
                v_pair = jnp.concatenate(v_done[hh] + [u[c:]] * (1 - cc), axis=0).astype(BF16)
                kd_t, decay_last = kd_ts[unit][cc]
                p45 = _dot(jnp.concatenate([attns[unit][cr], kd_t], axis=0), v_pair)
                state[hh] = s_olds[hh] * decay_last + p45[c:]
                o = p3s[hh][c:] + p45[:c]
                on = (o * lax.rsqrt(jnp.mean(o * o, axis=-1, keepdims=True) + NORM_EPS)
                      * ng_ref[...])
                cols = slice(hh * GDN_DIM, (hh + 1) * GDN_DIM)
                o_ref[orow, cols] = (on * _silu(z_ref[orow, cols])).astype(BF16)


def _gdn_core(qkv, z, bg, norm_g, *, batch, seq, tc):
    t, hw = z.shape
    tiles = seq // tc
    blk = lambda width: pl.BlockSpec((tc, width), lambda b, n: (b * tiles + n, 0))
    return pl.pallas_call(
        functools.partial(_gdn_core_body, tc=tc),
        grid=(batch, tiles),
        in_specs=[blk(3 * hw), blk(hw), blk(LANES), _resident((1, GDN_DIM))],
        out_specs=blk(hw),
        out_shape=jax.ShapeDtypeStruct((t, hw), BF16),
        scratch_shapes=[pltpu.VMEM((GDN_HEADS, GDN_DIM, GDN_DIM), F32)],
        compiler_params=_params(2),
        name="gdn_core",
    )(qkv, z, bg, norm_g)


def _gdn_out_body(h_ref, o_ref, w_ref, out_ref):
    out_ref[...] = h_ref[...] + _dot(o_ref[...], w_ref[...])


def _gdn_out(h, o, w_out, *, tm):
    t, d = h.shape
    row = lambda width: pl.BlockSpec((tm, width), lambda i: (i, 0))
    return pl.pallas_call(
        _gdn_out_body,
        grid=(t // tm,),
        in_specs=[row(d), row(o.shape[1]), _resident(w_out.shape)],
        out_specs=row(d),
        out_shape=jax.ShapeDtypeStruct((t, d), F32),
        compiler_params=_params(1),
        name="gdn_out",
    )(h, o, w_out)


def _tile(n, want):
    t = min(want, n)
    assert n % t == 0, (n, t)
    return t


def kernel(x, positions, ffn1_norm, ffn1_w_in, ffn1_w_out, mix_norm, ffn2_norm, ffn2_w_in,
           ffn2_w_out, hyb_w_in, hyb_dw_w, hyb_dw_b, hyb_ln_g, hyb_ln_b, hyb_w_out,
           gdn_w_in, gdn_conv_w, gdn_A_log, gdn_dt_bias, gdn_norm_g, gdn_w_out, final_norm):
    batch, seq, d = x.shape
    depth = ffn1_norm.shape[0]
    t = batch * seq
    h = x.reshape(t, d)
    pos = positions.reshape(t, 1)
    row = lambda a: a.reshape(1, -1).astype(F32)

    lane = jnp.arange(LANES) % A_HEAD_DIM
    inv_freq = jnp.power(jnp.float32(ROPE_THETA),
                         -jnp.arange(0, ROT_DIM, 2, dtype=F32) / ROT_DIM)
    half = ROT_DIM // 2
    invf = jnp.where(lane < ROT_DIM, inv_freq[lane % half], 0.0).astype(F32).reshape(1, LANES)
    shi = jnp.where((lane >= half) & (lane < ROT_DIM), 1.0, 0.0).astype(F32).reshape(1, LANES)
    slo = jnp.where(lane < half, -1.0, 0.0).astype(F32).reshape(1, LANES)
    bias = _attn_bias()

    gdn_hw = GDN_HEADS * GDN_DIM
    tm_ffn = _tile(t, 1024)
    tm_mix = _tile(seq, 512)
    final_g = row(final_norm)

    for layer in range(depth):
        i = layer // 2
        h = _ffn(h, row(ffn1_norm[layer]), ffn1_w_in[layer].astype(BF16),
                 ffn1_w_out[layer].astype(BF16), final_g, final_norm=False, tm=tm_ffn)
        if layer % 2 == 0:
            qkvs, glu = _hyb_in(h, row(mix_norm[layer]), hyb_w_in[i].astype(BF16), pos,
                                invf, shi, slo, tm=tm_mix)
            outs, lses = [], []
            for dil, (q, k, v) in zip(DILATIONS, qkvs):
                o_g, lse_g = _attn_branch(q, k, v, bias, dil, batch)
                outs.append(o_g)
                lses.append(lse_g)
            h = _hyb_out(h, outs, lses, glu, hyb_dw_w[i], row(hyb_dw_b[i]), row(hyb_ln_g[i]),
                         row(hyb_ln_b[i]), hyb_w_out[i].astype(BF16), tm=tm_mix, seq=seq)
        else:
            w_in = gdn_w_in[i]
            w_main = w_in[:, :4 * gdn_hw].astype(BF16)
            w_small = jnp.pad(w_in[:, 4 * gdn_hw:],
                              ((0, 0), (0, LANES - 2 * GDN_HEADS))).astype(BF16)
            lane_pad = lambda a: jnp.pad(a.astype(F32), (GDN_HEADS, LANES - 2 * GDN_HEADS)
                                         ).reshape(1, LANES)
            qkv, z, bg = _gdn_in(h, row(mix_norm[layer]), w_main, w_small, gdn_conv_w[i],
                                 lane_pad(gdn_A_log[i]), lane_pad(gdn_dt_bias[i]),
                                 tm=tm_mix, seq=seq)
            o = _gdn_core(qkv, z, bg, row(gdn_norm_g[i]), batch=batch, seq=seq,
                          tc=_tile(seq, 4 * GDN_CHUNK))
            h = _gdn_out(h, o, gdn_w_out[i].astype(BF16), tm=tm_ffn)
        h = _ffn(h, row(ffn2_norm[layer]), ffn2_w_in[layer].astype(BF16),
                 ffn2_w_out[layer].astype(BF16), final_g,
                 final_norm=(layer == depth - 1), tm=tm_ffn)
    return h.reshape(batch, seq, d)
```

```python
import functools

import jax
import jax.numpy as jnp
from jax import lax
from jax.experimental import pallas as pl
from jax.experimental.pallas import tpu as pltpu

F32 = jnp.float32
BF16 = jnp.bfloat16

NORM_EPS = 1e-6
ROPE_THETA = 500000.0
A_HEADS = 8
A_HEAD_DIM = 64
A_WIDTH = A_HEADS * A_HEAD_DIM
ROT_DIM = A_HEAD_DIM // 4
DILATIONS = (1, 4, 16)
WIN_BLOCK = 128
CONV_WIDTH = 31
GDN_HEADS = 8
GDN_DIM = 128
GDN_CHUNK = 64
GDN_SHORT_CONV = 4

LANES = 128
SUBLANES = 8
MXU_DIM = 256
VMEM_LIMIT = 56 * 1024 * 1024
MASK_VALUE = -1e30
LOG2_E = 1.4426950408889634


def _params(n_axes):
    return pltpu.CompilerParams(dimension_semantics=("arbitrary",) * n_axes,
                                vmem_limit_bytes=VMEM_LIMIT)


def _resident(shape):
    nd = len(shape)
    return pl.BlockSpec(shape, lambda *_: (0,) * nd, pipeline_mode=pl.Buffered(1))


def _rms_bf16(x, g):
    ms = jnp.mean(x * x, axis=-1, keepdims=True)
    return (x * lax.rsqrt(ms + NORM_EPS) * g).astype(BF16)


def _silu(x):
    return x * jax.nn.sigmoid(x)


def _dot(a, b):
    return jnp.dot(a, b, preferred_element_type=F32)


def _dot_nt(a, b):
    return lax.dot_general(a, b, (((1,), (1,)), ((), ())), preferred_element_type=F32)


def _dot_tn(a, b):
    return lax.dot_general(a, b, (((0,), (0,)), ((), ())), preferred_element_type=F32)


def _ffn_body(h_ref, g_ref, win_ref, wout_ref, fg_ref, o_ref, xn_ref, acc_ref, *, ffn, chunks,
              final_norm):
    xn_ref[...] = _rms_bf16(h_ref[...], g_ref[...])
    for n, (c0, cw) in enumerate(chunks):
        xn = xn_ref[...]
        gate = _dot(xn, win_ref[:, c0:c0 + cw])
        up = _dot(xn, win_ref[:, ffn + c0:ffn + c0 + cw])
        part = _dot((_silu(gate) * up).astype(BF16), wout_ref[c0:c0 + cw, :])
        if n == 0:
            acc_ref[...] = part
        else:
            acc_ref[...] += part
    out = h_ref[...] + 0.5 * acc_ref[...]
    if final_norm:
        ms = jnp.mean(out * out, axis=-1, keepdims=True)
        out = out * lax.rsqrt(ms + NORM_EPS) * fg_ref[...]
    o_ref[...] = out


def _ffn_chunks(ffn, width):
    out, c0 = [], 0
    while c0 < ffn:
        cw = min(width, ffn - c0)
        out.append((c0, cw))
        c0 += cw
    return tuple(out)


def _ffn(h, g, w_in, w_out, final_g, *, final_norm, tm):
    t, d = h.shape
    ffn = w_out.shape[0]
    body = functools.partial(_ffn_body, ffn=ffn, chunks=_ffn_chunks(ffn, 2 * MXU_DIM),
                             final_norm=final_norm)
    return pl.pallas_call(
        body,
        grid=(t // tm,),
        in_specs=[pl.BlockSpec((tm, d), lambda i: (i, 0)),
                  _resident((1, d)), _resident(w_in.shape), _resident(w_out.shape),
                  _resident((1, d))],
        out_specs=pl.BlockSpec((tm, d), lambda i: (i, 0)),
        out_shape=jax.ShapeDtypeStruct((t, d), F32),
        scratch_shapes=[pltpu.VMEM((tm, d), BF16), pltpu.VMEM((tm, d), F32)],
        compiler_params=_params(1),
        name="ffn_final" if final_norm else "ffn",
    )(h, g, w_in, w_out, final_g)


def _hyb_in_body(h_ref, g_ref, w_ref, pos_ref, invf_ref, shi_ref, slo_ref, *refs, tm):
    nd = len(DILATIONS)
    qkv_refs = [refs[3 * n:3 * n + 3] for n in range(nd)]
    glu_ref, stage = refs[3 * nd], refs[3 * nd + 1]
    xn = _rms_bf16(h_ref[...], g_ref[...])
    proj = _dot(xn, w_ref[...])
    ang = pos_ref[...].astype(F32) * invf_ref[...]
    cos, sin = jnp.cos(ang), jnp.sin(ang)
    reps = A_WIDTH // LANES
    cos4 = jnp.concatenate([cos] * reps, axis=1)
    sin_hi = jnp.concatenate([sin * shi_ref[...]] * reps, axis=1)
    sin_lo = jnp.concatenate([sin * slo_ref[...]] * reps, axis=1)
    half = ROT_DIM // 2

    def rot(x):
        return (x * cos4 + pltpu.roll(x, half, 1) * sin_hi
                + pltpu.roll(x, A_WIDTH - half, 1) * sin_lo)

    qkv = (rot(proj[:, :A_WIDTH]) * (A_HEAD_DIM ** -0.5 * LOG2_E),
           rot(proj[:, A_WIDTH:2 * A_WIDTH]),
           proj[:, 2 * A_WIDTH:3 * A_WIDTH])
    u = proj[:, 3 * A_WIDTH:]
    c = u.shape[1] // 2
    glu_ref[...] = u[:, :c] * jax.nn.sigmoid(u[:, c:])

    slabs = A_WIDTH // LANES
    for a, x in enumerate(qkv):
        for s in range(slabs):
            stage[a * slabs + s] = x[:, s * LANES:(s + 1) * LANES]
    for dil, out_refs in zip(DILATIONS, qkv_refs):
        for a, out_ref in enumerate(out_refs):
            if dil == 1:
                out_ref[...] = qkv[a].astype(BF16)
                continue
            for cls in range(dil):
                for s in range(slabs):
                    piece = stage[a * slabs + s, pl.ds(cls, tm // dil, stride=dil), :]
                    lane0 = cls * A_WIDTH + s * LANES
                    out_ref[:, lane0:lane0 + LANES] = piece.astype(BF16)


def _hyb_in(h, g, w, pos, invf, shi, slo, *, tm):
    t, d = h.shape
    cc = (w.shape[1] - 3 * A_WIDTH) // 2
    row = lambda width: pl.BlockSpec((tm, width), lambda i: (i, 0))
    dil_specs, dil_shapes = [], []
    for dil in DILATIONS:
        dil_specs += [pl.BlockSpec((tm // dil, dil * A_WIDTH), lambda i: (i, 0))] * 3
        dil_shapes += [jax.ShapeDtypeStruct((t // dil, dil * A_WIDTH), BF16)] * 3
    outs = pl.pallas_call(
        functools.partial(_hyb_in_body, tm=tm),
        grid=(t // tm,),
        in_specs=[row(d), _resident((1, d)), _resident(w.shape), row(1),
                  _resident((1, LANES)), _resident((1, LANES)), _resident((1, LANES))],
        out_specs=dil_specs + [row(cc)],
        out_shape=dil_shapes + [jax.ShapeDtypeStruct((t, cc), F32)],
        scratch_shapes=[pltpu.VMEM((3 * A_WIDTH // LANES, tm, LANES), F32)],
        compiler_params=_params(1),
        name="hyb_in",
    )(h, g, w, pos, invf, shi, slo)
    n = len(DILATIONS)
    return [tuple(outs[3 * i:3 * i + 3]) for i in range(n)], outs[3 * n]


def _attn_body(q_ref, kc_ref, kp_ref, vc_ref, vp_ref, bias_ref, o_ref, lse_ref, kcat, vcat, *, tq):
    i = pl.program_id(2)
    w = WIN_BLOCK
    kcat[0:w, :] = kp_ref[0]
    kcat[w:, :] = kc_ref[0]
    vcat[0:w, :] = vp_ref[0]
    vcat[w:, :] = vc_ref[0]
    slab = MXU_DIM
    hps = slab // A_HEAD_DIM
    lane_head = lax.broadcasted_iota(jnp.int32, (1, slab), 1) // A_HEAD_DIM
    lane_idx = lax.broadcasted_iota(jnp.int32, (1, LANES), 1)
    for j in range(tq // w):
        first = jnp.where(jnp.logical_and(i == 0, j == 0), 1, 0) if j == 0 else 0
        bias = bias_ref[first]
        lse_tile = jnp.zeros((w, LANES), F32)
        for s in range(A_WIDTH // slab):
            cols = slice(s * slab, (s + 1) * slab)
            qs = q_ref[0, j * w:(j + 1) * w, cols]
            lhs = jnp.concatenate(
                [jnp.where(lane_head == hh, qs, jnp.zeros_like(qs)) for hh in range(hps)], axis=0)
            sc = _dot_nt(lhs, kcat[j * w:(j + 2) * w, cols]) + bias
            m = jnp.max(sc, axis=1, keepdims=True)
            p = jnp.exp2(sc - m)
            den = jnp.sum(p, axis=1, keepdims=True)
            pv = _dot(p.astype(BF16), vcat[j * w:(j + 2) * w, cols]) * (1.0 / den)
            lse = m + jnp.log2(den)
            o_slab = jnp.zeros((w, slab), F32)
            for hh in range(hps):
                rows = slice(hh * w, (hh + 1) * w)
                o_slab = o_slab + jnp.where(lane_head == hh, pv[rows], 0.0)
                lse_tile = lse_tile + jnp.where(lane_idx == s * hps + hh, lse[rows], 0.0)
            o_ref[0, j * w:(j + 1) * w, cols] = o_slab
        lse_ref[0, j * w:(j + 1) * w, :] = lse_tile


def _attn_bias():
    w = WIN_BLOCK
    r = jnp.arange(w)[:, None]
    c = jnp.arange(2 * w)[None, :]
    band = (c >= r) & (c <= r + w)
    no_prev = band & (c >= w)
    bias = jnp.where(jnp.stack([band, no_prev]), 0.0, MASK_VALUE).astype(F32)
    return jnp.tile(bias, (1, MXU_DIM // A_HEAD_DIM, 1))


def _attn_branch(q, k, v, bias, dil, batch):
    width = A_WIDTH
    b = batch
    l = q.shape[0] // b
    w = WIN_BLOCK
    tq = min(4 * w, l)
    view = lambda a: a.reshape(b, l, dil * width)
    cur = pl.BlockSpec((1, tq, width), lambda bi, c, i: (bi, i, c))
    prev = pl.BlockSpec((1, w, width), lambda bi, c, i: (bi, jnp.maximum(i * (tq // w) - 1, 0), c))
    o, lse = pl.pallas_call(
        functools.partial(_attn_body, tq=tq),
        grid=(b, dil, l // tq),
        in_specs=[cur, cur, prev, cur, prev, _resident(bias.shape)],
        out_specs=[cur, pl.BlockSpec((1, tq, LANES), lambda bi, c, i: (bi, i, c))],
        out_shape=[jax.ShapeDtypeStruct((b, l, dil * width), F32),
                   jax.ShapeDtypeStruct((b, l, dil * LANES), F32)],
        scratch_shapes=[pltpu.VMEM((tq + w, width), BF16), pltpu.VMEM((tq + w, width), BF16)],
        compiler_params=_params(3),
        name=f"attn_d{dil}",
    )(view(q), view(k), view(k), view(v), view(v), bias)
    return o.reshape(b * l, dil * width), lse.reshape(b * l, dil * LANES)


def _conv_block(xpad, dww_ref, r0, nrows, lanes, halo):
    base = r0 + halo - SUBLANES
    acc = None
    for b in range(SUBLANES):
        z = None
        for a in range(-(-CONV_WIDTH // SUBLANES)):
            delay = SUBLANES * a + b
            if delay >= CONV_WIDTH:
                continue
            j = CONV_WIDTH - 1 - delay
            start = base - SUBLANES * a
            term = xpad[start:start + nrows + SUBLANES, lanes] * dww_ref[j:j + 1, lanes]
            z = term if z is None else z + term
        zs = z if b == 0 else pltpu.roll(z, b, 0)
        acc = zs if acc is None else acc + zs
    return acc[SUBLANES:]


def _hyb_out_body(h_ref, o1_ref, o2_ref, o3_ref, l1_ref, l2_ref, l3_ref, glu_ref, gprev_ref,
                  dww_ref, dwb_ref, lng_ref, lnb_ref, wout_ref, out_ref,
                  xpad, ostage, lstage, cat, *, tm, halo, tiles_per_seq, rb, sub):
    i = pl.program_id(0)
    slabs = A_WIDTH // LANES
    for n, (dil, o_ref, l_ref) in enumerate(zip(DILATIONS[1:], (o2_ref, o3_ref), (l2_ref, l3_ref))):
        for cls in range(dil):
            rows = pl.ds(cls, tm // dil, stride=dil)
            for s in range(slabs):
                lane0 = cls * A_WIDTH + s * LANES
                ostage[n * slabs + s, rows, :] = o_ref[:, lane0:lane0 + LANES]
            lstage[n, rows, :] = l_ref[:, cls * LANES:(cls + 1) * LANES]

    prev = gprev_ref[...]
    xpad[0:halo, :] = jnp.where(i % tiles_per_seq == 0, jnp.zeros_like(prev), prev)
    xpad[halo:, :] = glu_ref[...]

    lane = lax.broadcasted_iota(jnp.int32, (1, LANES), 1)
    heads_per_vreg = LANES // A_HEAD_DIM
    for blk in range(tm // rb):
        rows = slice(blk * rb, (blk + 1) * rb)
        l1, l2, l3 = l1_ref[rows, :], lstage[0, rows, :], lstage[1, rows, :]
        mx = jnp.maximum(jnp.maximum(l1, l2), l3)
        e1, e2, e3 = jnp.exp2(l1 - mx), jnp.exp2(l2 - mx), jnp.exp2(l3 - mx)
        tot = e1 + e2 + e3
        inv_tot = 1.0 / tot
        ws = (e1 * inv_tot, e2 * inv_tot, e3 * inv_tot)
        pieces = []
        for s in range(slabs):
            o_gs = (o1_ref[rows, s * LANES:(s + 1) * LANES], ostage[s, rows, :],
                    ostage[slabs + s, rows, :])
            acc = None
            for wg, o_g in zip(ws, o_gs):
                wfull = wg[:, s * heads_per_vreg:s * heads_per_vreg + 1]
                for hh in range(1, heads_per_vreg):
                    col = s * heads_per_vreg + hh
                    wfull = jnp.where(lane < hh * A_HEAD_DIM, wfull, wg[:, col:col + 1])
                term = wfull * o_g
                acc = term if acc is None else acc + term
            pieces.append(acc)

        cw = glu_ref.shape[1]
        conv = jnp.concatenate(
            [_conv_block(xpad, dww_ref, blk * rb, rb, slice(s * LANES, (s + 1) * LANES), halo)
             for s in range(cw // LANES)], axis=1) + dwb_ref[...]
        mu = jnp.mean(conv, axis=-1, keepdims=True)
        xc = conv - mu
        y = xc * lax.rsqrt(jnp.mean(xc * xc, axis=-1, keepdims=True) + NORM_EPS)
        c = _silu(y * lng_ref[...] + lnb_ref[...])
        cat[rows, :] = jnp.concatenate(pieces + [c], axis=1).astype(BF16)

        if (blk + 1) % (sub // rb) == 0:
            srows = slice((blk + 1) * rb - sub, (blk + 1) * rb)
            out_ref[srows, :] = h_ref[srows, :] + _dot(cat[srows, :], wout_ref[...])


def _hyb_out(h, os_, lses, glu, dw_w, dw_b, ln_g, ln_b, w_out, *, tm, seq):
    t, d = h.shape
    cc = glu.shape[1]
    halo = 4 * SUBLANES
    assert DILATIONS[0] == 1 and halo >= SUBLANES * -(-CONV_WIDTH // SUBLANES)
    row = lambda width: pl.BlockSpec((tm, width), lambda i: (i, 0))
    dil_row = lambda width: [pl.BlockSpec((tm // dil, dil * width), lambda i: (i, 0))
                             for dil in DILATIONS]
    prev = pl.BlockSpec((halo, cc), lambda i: (jnp.maximum(i * (tm // halo) - 1, 0), 0))
    body = functools.partial(_hyb_out_body, tm=tm, halo=halo, tiles_per_seq=seq // tm,
                             rb=4 * SUBLANES, sub=min(tm, MXU_DIM))
    nd = len(DILATIONS) - 1
    return pl.pallas_call(
        body,
        grid=(t // tm,),
        in_specs=[row(d)] + dil_row(A_WIDTH) + dil_row(LANES) + [row(cc), prev,
                  _resident(dw_w.shape), _resident((1, cc)), _resident((1, cc)),
                  _resident((1, cc)), _resident(w_out.shape)],
        out_specs=row(d),
        out_shape=jax.ShapeDtypeStruct((t, d), F32),
        scratch_shapes=[pltpu.VMEM((tm + halo, cc), F32),
                        pltpu.VMEM((nd * A_WIDTH // LANES, tm, LANES), F32),
                        pltpu.VMEM((nd, tm, LANES), F32),
                        pltpu.VMEM((tm, d), BF16)],
        compiler_params=_params(1),
        name="hyb_out",
    )(h, *os_, *lses, glu, glu, dw_w, dw_b, ln_g, ln_b, w_out)


def _gdn_in_body(h_ref, g_ref, w_ref, ws_ref, cw_ref, alog_ref, dtb_ref,
                 qkv_ref, z_ref, bg_ref, xn_ref, carry, *, tm, tiles_per_seq, chunk, rb):
    i = pl.program_id(0)
    xn_ref[...] = _rms_bf16(h_ref[...], g_ref[...])
    hw = GDN_HEADS * GDN_DIM
    halo = SUBLANES
    taps = GDN_SHORT_CONV

    @pl.when(i % tiles_per_seq == 0)
    def _():
        carry[...] = jnp.zeros_like(carry)

    def l2n(x):
        return x * lax.rsqrt(jnp.sum(x * x, axis=-1, keepdims=True) + NORM_EPS)

    def finish_qkv(c0, pc):
        for s0 in range(0, chunk, GDN_DIM):
            lanes = slice(c0 + s0, c0 + s0 + GDN_DIM)
            w0, w1, w2, w3 = (cw_ref[j:j + 1, lanes] for j in range(taps))
            for r0 in range(0, tm, rb):
                top = carry[:, lanes] if r0 == 0 else pc[r0 - halo:r0, s0:s0 + GDN_DIM]
                ext = jnp.concatenate([top, pc[r0:r0 + rb, s0:s0 + GDN_DIM]], axis=0)
                x1 = pltpu.roll(ext, 1, 0)
                even = ext * w3 + x1 * w2
                odd = ext * w1 + x1 * w0
                y = _silu((even + pltpu.roll(odd, 2, 0))[halo:])
                qkv_ref[r0:r0 + rb, lanes] = l2n(y) if c0 < 2 * hw else y
            carry[:, lanes] = pc[tm - halo:, s0:s0 + GDN_DIM]

    qkv_chunks = list(range(0, 3 * hw, chunk))
    z_chunks = list(range(3 * hw, 4 * hw, chunk))
    every = -(-len(qkv_chunks) // max(len(z_chunks), 1))
    order = []
    for n, c0 in enumerate(qkv_chunks):
        order.append(c0)
        if (n + 1) % every == 0 and z_chunks:
            order.append(z_chunks.pop(0))
    order += z_chunks
    pending = None
    for c0 in order + [None]:
        nxt = None if c0 is None else _dot(xn_ref[...], w_ref[:, c0:c0 + chunk])
        if pending is not None:
            p0, pc = pending
            if p0 < 3 * hw:
                finish_qkv(p0, pc)
            else:
                z_ref[:, p0 - 3 * hw:p0 - 3 * hw + chunk] = pc
        pending = (c0, nxt)

    small = _dot(xn_ref[...], ws_ref[...])
    x = small + dtb_ref[...]
    softplus = jnp.maximum(x, 0.0) + jnp.log1p(jnp.exp(-jnp.abs(x)))
    lane = lax.broadcasted_iota(jnp.int32, (1, LANES), 1)
    bg_ref[...] = jnp.where(lane < GDN_HEADS, jax.nn.sigmoid(small),
                            -jnp.exp(alog_ref[...]) * softplus)


def _gdn_in(h, g, w_main, w_small, conv_w, alog, dtb, *, tm, seq):
    t, d = h.shape
    hw = GDN_HEADS * GDN_DIM
    row = lambda width: pl.BlockSpec((tm, width), lambda i: (i, 0))
    body = functools.partial(_gdn_in_body, tm=tm, tiles_per_seq=seq // tm, chunk=2 * MXU_DIM,
                             rb=8 * SUBLANES)
    return pl.pallas_call(
        body,
        grid=(t // tm,),
        in_specs=[row(d), _resident((1, d)),
                  _resident(w_main.shape), _resident(w_small.shape), _resident(conv_w.shape),
                  _resident((1, LANES)), _resident((1, LANES))],
        out_specs=[row(3 * hw), row(hw), row(LANES)],
        out_shape=[jax.ShapeDtypeStruct((t, 3 * hw), F32), jax.ShapeDtypeStruct((t, hw), F32),
                   jax.ShapeDtypeStruct((t, LANES), F32)],
        scratch_shapes=[pltpu.VMEM((tm, d), BF16), pltpu.VMEM((SUBLANES, 3 * hw), F32)],
        compiler_params=_params(1),
        name="gdn_in",
    )(h, g, w_main, w_small, conv_w, alog, dtb)


def _split2(a):
    hi = a.astype(BF16)
    lo = (a - hi.astype(F32)).astype(BF16)
    return hi, lo


def _split_dot(a_parts, b_parts):
    a_hi, a_lo = a_parts
    b_hi, b_lo = b_parts
    return _dot(jnp.concatenate([a_hi, a_hi, a_lo], axis=1),
                jnp.concatenate([b_hi, b_lo, b_hi], axis=0))


def _unit_lower_inverses(lms, eye, nilpotency):
    n = lms[0].shape[0]
    pows = [-lm for lm in lms]
    sums = [eye + p for p in pows]
    pows = [_split_dot(ps, ps) for ps in (_split2(p) for p in pows)]
    m = 2
    while 2 * m < nilpotency:
        prods = []
        for p, s in zip(pows, sums):
            (p_hi, p_lo), (s_hi, s_lo) = _split2(p), _split2(s)
            wide = (jnp.concatenate([p_hi, s_hi], axis=1), jnp.concatenate([p_lo, s_lo], axis=1))
            prods.append(_split_dot((p_hi, p_lo), wide))
        sums = [s + r[:, n:] for s, r in zip(sums, prods)]
        pows = [r[:, :n] for r in prods]
        m *= 2
    return [s + _split_dot(_split2(p), _split2(s)) for s, p in zip(sums, pows)]


def _block_diag(a, b):
    zero = jnp.zeros_like(a)
    return jnp.concatenate([jnp.concatenate([a, zero], axis=1),
                            jnp.concatenate([zero, b], axis=1)], axis=0)


def _gdn_core_body(qkv_ref, z_ref, bg_ref, ng_ref, o_ref, state, *, tc):
    n = pl.program_id(1)
    c = GDN_CHUNK
    pr = 2 * c
    hw = GDN_HEADS * GDN_DIM
    dim = GDN_DIM
    scale = GDN_DIM ** -0.5
    head_pairs = GDN_HEADS // 2

    @pl.when(n == 0)
    def _():
        state[...] = jnp.zeros_like(state)

    ri = lax.broadcasted_iota(jnp.int32, (tc, tc), 0)
    ci = lax.broadcasted_iota(jnp.int32, (tc, tc), 1)
    tri_l = jnp.logical_and(ri // c == ci // c, ri >= ci).astype(BF16)
    bg = bg_ref[...]
    g1 = bg.astype(BF16)
    r1 = bg - g1.astype(F32)
    g2 = r1.astype(BF16)
    g3 = (r1 - g2.astype(F32)).astype(BF16)
    gc = _dot(jnp.concatenate([tri_l] * 3, axis=1), jnp.concatenate([g1, g2, g3], axis=0))
    gct = gc.T

    pi = lax.broadcasted_iota(jnp.int32, (pr, pr), 0)
    pj = lax.broadcasted_iota(jnp.int32, (pr, pr), 1)
    same_chunk = pi // c == pj // c
    eye_f = (pi == pj).astype(F32)
    causal = jnp.logical_and(same_chunk, pi >= pj)
    strict = jnp.logical_and(same_chunk, pi > pj)
    row_chunk = lax.broadcasted_iota(jnp.int32, (pr, 1), 0) // c

    lms, attns, rhs2s, q_decs, kd_ts = [], [], [], [], []
    for pp in range(tc // pr):
        rows = slice(pp * pr, (pp + 1) * pr)
        for hp in range(head_pairs):
            lhs1, ks, decays = [], [], []
            for hh in (2 * hp, 2 * hp + 1):
                q = qkv_ref[rows, hh * dim:(hh + 1) * dim] * scale
                k = qkv_ref[rows, hw + hh * dim:hw + (hh + 1) * dim]
                v = qkv_ref[rows, 2 * hw + hh * dim:2 * hw + (hh + 1) * dim]
                b_col = bg[rows, hh:hh + 1]
                g_col = gc[rows, GDN_HEADS + hh:GDN_HEADS + hh + 1]
                g_row = gct[GDN_HEADS + hh:GDN_HEADS + hh + 1, rows]
                decays.append(jnp.exp(jnp.where(causal, g_col - g_row, MASK_VALUE)))
                eg = jnp.exp(g_col)
                kb = k * b_col
                lhs1.append(jnp.concatenate([kb, q], axis=0).astype(BF16))
                ks.append(k.astype(BF16))
                rhs2s.append(jnp.concatenate([v * b_col, kb * eg], axis=1).astype(BF16))
                q_decs.append((q * eg).astype(BF16))
                kds = []
                for cc in range(2):
                    last = pp * pr + cc * c + c - 1
                    g_last = gc[last:last + 1, GDN_HEADS + hh:GDN_HEADS + hh + 1]
                    kd = jnp.where(row_chunk == cc, k * jnp.exp(g_last - g_col), 0.0)
                    kds.append((kd.T.astype(BF16), jnp.exp(g_last)))
                kd_ts.append(kds)
            p1 = _dot_nt(jnp.concatenate(lhs1, axis=1), _block_diag(ks[0], ks[1]))
            for e in range(2):
                blk = p1[:, e * pr:(e + 1) * pr]
                lms.append(jnp.where(strict, blk[:pr] * decays[e], 0.0))
                attns.append((blk[pr:] * decays[e]).astype(BF16))

    t_invs = _unit_lower_inverses(lms, eye_f, c)
    p2s = [_dot(t.astype(BF16), r) for t, r in zip(t_invs, rhs2s)]

    for pp in range(tc // pr):
        v_done = {hh: [] for hh in range(GDN_HEADS)}
        for cc in range(2):
            cr = slice(cc * c, (cc + 1) * c)
            orow = slice(pp * pr + cc * c, pp * pr + (cc + 1) * c)
            s_olds = [state[hh] for hh in range(GDN_HEADS)]
            p3s = []
            for hp in range(head_pairs):
                lhs = []
                for hh in (2 * hp, 2 * hp + 1):
                    unit = pp * GDN_HEADS + hh
                    w = p2s[unit][cr, dim:]
                    lhs.append(jnp.concatenate([w.astype(BF16), q_decs[unit][cr]], axis=0))
                s_bd = _block_diag(s_olds[2 * hp].astype(BF16), s_olds[2 * hp + 1].astype(BF16))
                p3s.append(_dot(jnp.concatenate(lhs, axis=1), s_bd))
            for hp in range(head_pairs):
                lhs, v_pairs = [], []
                for e, hh in enumerate((2 * hp, 2 * hp + 1)):
                    unit = pp * GDN_HEADS + hh
                    u = p2s[unit][:, :dim]
                    v_new = u[cr] - p3s[hp][:c, e * dim:(e + 1) * dim]
                    v_done[hh].append(v_new)
                    v_pairs.append(jnp.concatenate(v_done[hh] + [u[c:]] * (1 - cc),
                                                   axis=0).astype(BF16))
                    lhs.append(jnp.concatenate([attns[unit][cr], kd_ts[unit][cc][0]], axis=0))
                p45 = _dot(jnp.concatenate(lhs, axis=1), _block_diag(v_pairs[0], v_pairs[1]))
                for e, hh in enumerate((2 * hp, 2 * hp + 1)):
                    unit = pp * GDN_HEADS + hh
                    lanes = slice(e * dim, (e + 1) * dim)
                    state[hh] = s_olds[hh] * kd_ts[unit][cc][1] + p45[c:, lanes]
                    o = p3s[hp][c:, lanes] + p45[:c, lanes]
                    on = (o * lax.rsqrt(jnp.mean(o * o, axis=-1, keepdims=True) + NORM_EPS)
                          * ng_ref[...])
                    cols = slice(hh * dim, (hh + 1) * dim)
                    o_ref[orow, cols] = (on * _silu(z_ref[orow, cols])).astype(BF16)


def _gdn_core(qkv, z, bg, norm_g, *, batch, seq, tc):
    t, hw = z.shape
    tiles = seq // tc
    blk = lambda width: pl.BlockSpec((tc, width), lambda b, n: (b * tiles + n, 0))
    return pl.pallas_call(
        functools.partial(_gdn_core_body, tc=tc),
        grid=(batch, tiles),
        in_specs=[blk(3 * hw), blk(hw), blk(LANES), _resident((1, GDN_DIM))],
        out_specs=blk(hw),
        out_shape=jax.ShapeDtypeStruct((t, hw), BF16),
        scratch_shapes=[pltpu.VMEM((GDN_HEADS, GDN_DIM, GDN_DIM), F32)],
        compiler_params=_params(2),
        name="gdn_core",
    )(qkv, z, bg, norm_g)


def _gdn_out_body(h_ref, o_ref, w_ref, out_ref):
    out_ref[...] = h_ref[...] + _dot(o_ref[...], w_ref[...])


def _gdn_out(h, o, w_out, *, tm):
    t, d = h.shape
    row = lambda width: pl.BlockSpec((tm, width), lambda i: (i, 0))
    return pl.pallas_call(
        _gdn_out_body,
        grid=(t // tm,),
        in_specs=[row(d), row(o.shape[1]), _resident(w_out.shape)],
        out_specs=row(d),
        out_shape=jax.ShapeDtypeStruct((t, d), F32),
        compiler_params=_params(1),
        name="gdn_out",
    )(h, o, w_out)


def _tile(n, want):
    t = min(want, n)
    assert n % t == 0, (n, t)
    return t


def kernel(x, positions, ffn1_norm, ffn1_w_in, ffn1_w_out, mix_norm, ffn2_norm, ffn2_w_in,
           ffn2_w_out, hyb_w_in, hyb_dw_w, hyb_dw_b, hyb_ln_g, hyb_ln_b, hyb_w_out,
           gdn_w_in, gdn_conv_w, gdn_A_log, gdn_dt_bias, gdn_norm_g, gdn_w_out, final_norm):
    batch, seq, d = x.shape
    depth = ffn1_norm.shape[0]
    t = batch * seq
    h = x.reshape(t, d)
    pos = positions.reshape(t, 1)
    row = lambda a: a.reshape(1, -1).astype(F32)

    lane = jnp.arange(LANES) % A_HEAD_DIM
    inv_freq = jnp.power(jnp.float32(ROPE_THETA),
                         -jnp.arange(0, ROT_DIM, 2, dtype=F32) / ROT_DIM)
    half = ROT_DIM // 2
    invf = jnp.where(lane < ROT_DIM, inv_freq[lane % half], 0.0).astype(F32).reshape(1, LANES)
    shi = jnp.where((lane >= half) & (lane < ROT_DIM), 1.0, 0.0).astype(F32).reshape(1, LANES)
    slo = jnp.where(lane < half, -1.0, 0.0).astype(F32).reshape(1, LANES)
    bias = _attn_bias()

    gdn_hw = GDN_HEADS * GDN_DIM
    tm_ffn = _tile(t, 1024)
    tm_mix = _tile(seq, 512)
    final_g = row(final_norm)

    for layer in range(depth):
        i = layer // 2
        h = _ffn(h, row(ffn1_norm[layer]), ffn1_w_in[layer].astype(BF16),
                 ffn1_w_out[layer].astype(BF16), final_g, final_norm=False, tm=tm_ffn)
        if layer % 2 == 0:
            qkvs, glu = _hyb_in(h, row(mix_norm[layer]), hyb_w_in[i].astype(BF16), pos,
                                invf, shi, slo, tm=tm_mix)
            outs, lses = [], []
            for dil, (q, k, v) in zip(DILATIONS, qkvs):
                o_g, lse_g = _attn_branch(q, k, v, bias, dil, batch)
                outs.append(o_g)
                lses.append(lse_g)
            h = _hyb_out(h, outs, lses, glu, hyb_dw_w[i], row(hyb_dw_b[i]), row(hyb_ln_g[i]),
                         row(hyb_ln_b[i]), hyb_w_out[i].astype(BF16), tm=tm_mix, seq=seq)
        else:
            w_in = gdn_w_in[i]
            w_main = w_in[:, :4 * gdn_hw].astype(BF16)
            w_small = jnp.pad(w_in[:, 4 * gdn_hw:],
                              ((0, 0), (0, LANES - 2 * GDN_HEADS))).astype(BF16)
            lane_pad = lambda a: jnp.pad(a.astype(F32), (GDN_HEADS, LANES - 2 * GDN_HEADS)
                                         ).reshape(1, LANES)
            qkv, z, bg = _gdn_in(h, row(mix_norm[layer]), w_main, w_small, gdn_conv_w[i],
                                 lane_pad(gdn_A_log[i]), lane_pad(gdn_dt_bias[i]),
                                 tm=tm_mix, seq=seq)
            o = _gdn_core(qkv, z, bg, row(gdn_norm_g[i]), batch=batch, seq=seq,
                          tc=_tile(seq, 4 * GDN_CHUNK))
            h = _gdn_out(h, o, gdn_w_out[i].astype(BF16), tm=tm_ffn)
        h = _ffn(h, row(ffn2_norm[layer]), ffn2_w_in[layer].astype(BF16),
                 ffn2_w_out[layer].astype(BF16), final_g,
                 final_norm=(layer == depth - 1), tm=tm_ffn)
    return h.reshape(batch, seq, d)
```

```python
import functools

import jax
import jax.numpy as jnp
from jax import lax
from jax.experimental import pallas as pl
from jax.experimental.pallas import tpu as pltpu

F32 = jnp.float32
BF16 = jnp.bfloat16

NORM_EPS = 1e-6
ROPE_THETA = 500000.0
A_HEADS = 8
A_HEAD_DIM = 64
A_WIDTH = A_HEADS * A_HEAD_DIM
ROT_DIM = A_HEAD_DIM // 4
DILATIONS = (1, 4, 16)
WIN_BLOCK = 128
CONV_WIDTH = 31
GDN_HEADS = 8
GDN_DIM = 128
GDN_CHUNK = 64
GDN_SHORT_CONV = 4

LANES = 128
SUBLANES = 8
MXU_DIM = 256
VMEM_LIMIT = 56 * 1024 * 1024
MASK_VALUE = -1e30
LOG2_E = 1.4426950408889634


def _params(n_axes):
    return pltpu.CompilerParams(dimension_semantics=("arbitrary",) * n_axes,
                                vmem_limit_bytes=VMEM_LIMIT)


def _resident(shape):
    nd = len(shape)
    return pl.BlockSpec(shape, lambda *_: (0,) * nd, pipeline_mode=pl.Buffered(1))


def _rms_bf16(x, g):
    ms = jnp.mean(x * x, axis=-1, keepdims=True)
    return (x * lax.rsqrt(ms + NORM_EPS) * g).astype(BF16)


def _silu(x):
    return x * jax.nn.sigmoid(x)


def _dot(a, b):
    return jnp.dot(a, b, preferred_element_type=F32)


def _dot_nt(a, b):
    return lax.dot_general(a, b, (((1,), (1,)), ((), ())), preferred_element_type=F32)


def _dot_tn(a, b):
    return lax.dot_general(a, b, (((0,), (0,)), ((), ())), preferred_element_type=F32)


def _ffn_body(h_ref, g_ref, win_ref, wout_ref, fg_ref, *refs, ffn, chunks, final_norm, mixed):
    if mixed:
        mix_ref, wmix_ref, o_ref, xn_ref, acc_ref, hres_ref = refs
        hres_ref[...] = h_ref[...] + _dot(mix_ref[...], wmix_ref[...])
        h_ref = hres_ref
    else:
        o_ref, xn_ref, acc_ref = refs
    xn_ref[...] = _rms_bf16(h_ref[...], g_ref[...])
    for n, (c0, cw) in enumerate(chunks):
        xn = xn_ref[...]
        gate = _dot(xn, win_ref[:, c0:c0 + cw])
        up = _dot(xn, win_ref[:, ffn + c0:ffn + c0 + cw])
        part = _dot((_silu(gate) * up).astype(BF16), wout_ref[c0:c0 + cw, :])
        if n == 0:
            acc_ref[...] = part
        else:
            acc_ref[...] += part
    out = h_ref[...] + 0.5 * acc_ref[...]
    if final_norm:
        ms = jnp.mean(out * out, axis=-1, keepdims=True)
        out = out * lax.rsqrt(ms + NORM_EPS) * fg_ref[...]
    o_ref[...] = out


def _ffn_chunks(ffn, width):
    out, c0 = [], 0
    while c0 < ffn:
        cw = min(width, ffn - c0)
        out.append((c0, cw))
        c0 += cw
    return tuple(out)


def _ffn(h, g, w_in, w_out, final_g, *, final_norm, tm, mix=None):
    t, d = h.shape
    ffn = w_out.shape[0]
    body = functools.partial(_ffn_body, ffn=ffn, chunks=_ffn_chunks(ffn, 2 * MXU_DIM),
                             final_norm=final_norm, mixed=mix is not None)
    row = lambda width: pl.BlockSpec((tm, width), lambda i: (i, 0))
    in_specs = [row(d), _resident((1, d)), _resident(w_in.shape), _resident(w_out.shape),
                _resident((1, d))]
    scratch = [pltpu.VMEM((tm, d), BF16), pltpu.VMEM((tm, d), F32)]
    args = [h, g, w_in, w_out, final_g]
    if mix is not None:
        in_specs += [row(mix[0].shape[1]), _resident(mix[1].shape)]
        scratch.append(pltpu.VMEM((tm, d), F32))
        args += list(mix)
    return pl.pallas_call(
        body,
        grid=(t // tm,),
        in_specs=in_specs,
        out_specs=row(d),
        out_shape=jax.ShapeDtypeStruct((t, d), F32),
        scratch_shapes=scratch,
        compiler_params=_params(1),
        name="ffn_final" if final_norm else ("ffn_mix" if mix is not None else "ffn"),
    )(*args)


def _hyb_in_body(h_ref, g_ref, w_ref, pos_ref, invf_ref, shi_ref, slo_ref, *refs, tm):
    nd = len(DILATIONS)
    qkv_refs = [refs[3 * n:3 * n + 3] for n in range(nd)]
    glu_ref, stage = refs[3 * nd], refs[3 * nd + 1]
    xn = _rms_bf16(h_ref[...], g_ref[...])
    proj = _dot(xn, w_ref[...])
    ang = pos_ref[...].astype(F32) * invf_ref[...]
    cos, sin = jnp.cos(ang), jnp.sin(ang)
    reps = A_WIDTH // LANES
    cos4 = jnp.concatenate([cos] * reps, axis=1)
    sin_hi = jnp.concatenate([sin * shi_ref[...]] * reps, axis=1)
    sin_lo = jnp.concatenate([sin * slo_ref[...]] * reps, axis=1)
    half = ROT_DIM // 2

    def rot(x):
        return (x * cos4 + pltpu.roll(x, half, 1) * sin_hi
                + pltpu.roll(x, A_WIDTH - half, 1) * sin_lo)

    qkv = (rot(proj[:, :A_WIDTH]) * (A_HEAD_DIM ** -0.5 * LOG2_E),
           rot(proj[:, A_WIDTH:2 * A_WIDTH]),
           proj[:, 2 * A_WIDTH:3 * A_WIDTH])
    u = proj[:, 3 * A_WIDTH:]
    c = u.shape[1] // 2
    glu_ref[...] = u[:, :c] * jax.nn.sigmoid(u[:, c:])

    slabs = A_WIDTH // LANES
    for a, x in enumerate(qkv):
        for s in range(slabs):
            stage[a * slabs + s] = x[:, s * LANES:(s + 1) * LANES]
    for dil, out_refs in zip(DILATIONS, qkv_refs):
        for a, out_ref in enumerate(out_refs):
            if dil == 1:
                out_ref[...] = qkv[a].astype(BF16)
                continue
            for cls in range(dil):
                for s in range(slabs):
                    piece = stage[a * slabs + s, pl.ds(cls, tm // dil, stride=dil), :]
                    lane0 = cls * A_WIDTH + s * LANES
                    out_ref[:, lane0:lane0 + LANES] = piece.astype(BF16)


def _hyb_in(h, g, w, pos, invf, shi, slo, *, tm):
    t, d = h.shape
    cc = (w.shape[1] - 3 * A_WIDTH) // 2
    row = lambda width: pl.BlockSpec((tm, width), lambda i: (i, 0))
    dil_specs, dil_shapes = [], []
    for dil in DILATIONS:
        dil_specs += [pl.BlockSpec((tm // dil, dil * A_WIDTH), lambda i: (i, 0))] * 3
        dil_shapes += [jax.ShapeDtypeStruct((t // dil, dil * A_WIDTH), BF16)] * 3
    outs = pl.pallas_call(
        functools.partial(_hyb_in_body, tm=tm),
        grid=(t // tm,),
        in_specs=[row(d), _resident((1, d)), _resident(w.shape), row(1),
                  _resident((1, LANES)), _resident((1, LANES)), _resident((1, LANES))],
        out_specs=dil_specs + [row(cc)],
        out_shape=dil_shapes + [jax.ShapeDtypeStruct((t, cc), F32)],
        scratch_shapes=[pltpu.VMEM((3 * A_WIDTH // LANES, tm, LANES), F32)],
        compiler_params=_params(1),
        name="hyb_in",
    )(h, g, w, pos, invf, shi, slo)
    n = len(DILATIONS)
    return [tuple(outs[3 * i:3 * i + 3]) for i in range(n)], outs[3 * n]


def _attn_body(q_ref, kc_ref, kp_ref, vc_ref, vp_ref, bias_ref, o_ref, lse_ref, kcat, vcat, *, tq):
    i = pl.program_id(2)
    w = WIN_BLOCK
    kcat[0:w, :] = kp_ref[0]
    kcat[w:, :] = kc_ref[0]
    vcat[0:w, :] = vp_ref[0]
    vcat[w:, :] = vc_ref[0]
    slab = MXU_DIM
    hps = slab // A_HEAD_DIM
    lane_head = lax.broadcasted_iota(jnp.int32, (1, slab), 1) // A_HEAD_DIM
    lane_idx = lax.broadcasted_iota(jnp.int32, (1, LANES), 1)
    for j in range(tq // w):
        first = jnp.where(jnp.logical_and(i == 0, j == 0), 1, 0) if j == 0 else 0
        bias = bias_ref[first]
        lse_tile = jnp.zeros((w, LANES), F32)
        for s in range(A_WIDTH // slab):
            cols = slice(s * slab, (s + 1) * slab)
            qs = q_ref[0, j * w:(j + 1) * w, cols]
            lhs = jnp.concatenate(
                [jnp.where(lane_head == hh, qs, jnp.zeros_like(qs)) for hh in range(hps)], axis=0)
            sc = _dot_nt(lhs, kcat[j * w:(j + 2) * w, cols]) + bias
            m = jnp.max(sc, axis=1, keepdims=True)
            p = jnp.exp2(sc - m)
            den = jnp.sum(p, axis=1, keepdims=True)
            pv = _dot(p.astype(BF16), vcat[j * w:(j + 2) * w, cols]) * (1.0 / den)
            lse = m + jnp.log2(den)
            o_slab = jnp.zeros((w, slab), F32)
            for hh in range(hps):
                rows = slice(hh * w, (hh + 1) * w)
                o_slab = o_slab + jnp.where(lane_head == hh, pv[rows], 0.0)
                lse_tile = lse_tile + jnp.where(lane_idx == s * hps + hh, lse[rows], 0.0)
            o_ref[0, j * w:(j + 1) * w, cols] = o_slab
        lse_ref[0, j * w:(j + 1) * w, :] = lse_tile


def _attn_bias():
    w = WIN_BLOCK
    r = jnp.arange(w)[:, None]
    c = jnp.arange(2 * w)[None, :]
    band = (c >= r) & (c <= r + w)
    no_prev = band & (c >= w)
    bias = jnp.where(jnp.stack([band, no_prev]), 0.0, MASK_VALUE).astype(F32)
    return jnp.tile(bias, (1, MXU_DIM // A_HEAD_DIM, 1))


def _attn_branch(q, k, v, bias, dil, batch):
    width = A_WIDTH
    b = batch
    l = q.shape[0] // b
    w = WIN_BLOCK
    tq = min(4 * w, l)
    view = lambda a: a.reshape(b, l, dil * width)
    cur = pl.BlockSpec((1, tq, width), lambda bi, c, i: (bi, i, c))
    prev = pl.BlockSpec((1, w, width), lambda bi, c, i: (bi, jnp.maximum(i * (tq // w) - 1, 0), c))
    o, lse = pl.pallas_call(
        functools.partial(_attn_body, tq=tq),
        grid=(b, dil, l // tq),
        in_specs=[cur, cur, prev, cur, prev, _resident(bias.shape)],
        out_specs=[cur, pl.BlockSpec((1, tq, LANES), lambda bi, c, i: (bi, i, c))],
        out_shape=[jax.ShapeDtypeStruct((b, l, dil * width), F32),
                   jax.ShapeDtypeStruct((b, l, dil * LANES), F32)],
        scratch_shapes=[pltpu.VMEM((tq + w, width), BF16), pltpu.VMEM((tq + w, width), BF16)],
        compiler_params=_params(3),
        name=f"attn_d{dil}",
    )(view(q), view(k), view(k), view(v), view(v), bias)
    return o.reshape(b * l, dil * width), lse.reshape(b * l, dil * LANES)


def _conv_block(xpad, dww_ref, r0, nrows, lanes, halo):
    base = r0 + halo - SUBLANES
    acc = None
    for b in range(SUBLANES):
        z = None
        for a in range(-(-CONV_WIDTH // SUBLANES)):
            delay = SUBLANES * a + b
            if delay >= CONV_WIDTH:
                continue
            j = CONV_WIDTH - 1 - delay
            start = base - SUBLANES * a
            term = xpad[start:start + nrows + SUBLANES, lanes] * dww_ref[j:j + 1, lanes]
            z = term if z is None else z + term
        zs = z if b == 0 else pltpu.roll(z, b, 0)
        acc = zs if acc is None else acc + zs
    return acc[SUBLANES:]


def _hyb_out_body(h_ref, o1_ref, o2_ref, o3_ref, l1_ref, l2_ref, l3_ref, glu_ref, gprev_ref,
                  dww_ref, dwb_ref, lng_ref, lnb_ref, wout_ref, out_ref,
                  xpad, ostage, lstage, cat, *, tm, halo, tiles_per_seq, rb, sub):
    i = pl.program_id(0)
    slabs = A_WIDTH // LANES
    for n, (dil, o_ref, l_ref) in enumerate(zip(DILATIONS[1:], (o2_ref, o3_ref), (l2_ref, l3_ref))):
        for cls in range(dil):
            rows = pl.ds(cls, tm // dil, stride=dil)
            for s in range(slabs):
                lane0 = cls * A_WIDTH + s * LANES
                ostage[n * slabs + s, rows, :] = o_ref[:, lane0:lane0 + LANES]
            lstage[n, rows, :] = l_ref[:, cls * LANES:(cls + 1) * LANES]

    prev = gprev_ref[...]
    xpad[0:halo, :] = jnp.where(i % tiles_per_seq == 0, jnp.zeros_like(prev), prev)
    xpad[halo:, :] = glu_ref[...]

    lane = lax.broadcasted_iota(jnp.int32, (1, LANES), 1)
    heads_per_vreg = LANES // A_HEAD_DIM
    for blk in range(tm // rb):
        rows = slice(blk * rb, (blk + 1) * rb)
        l1, l2, l3 = l1_ref[rows, :], lstage[0, rows, :], lstage[1, rows, :]
        mx = jnp.maximum(jnp.maximum(l1, l2), l3)
        e1, e2, e3 = jnp.exp2(l1 - mx), jnp.exp2(l2 - mx), jnp.exp2(l3 - mx)
        tot = e1 + e2 + e3
        inv_tot = 1.0 / tot
        ws = (e1 * inv_tot, e2 * inv_tot, e3 * inv_tot)
        pieces = []
        for s in range(slabs):
            o_gs = (o1_ref[rows, s * LANES:(s + 1) * LANES], ostage[s, rows, :],
                    ostage[slabs + s, rows, :])
            acc = None
            for wg, o_g in zip(ws, o_gs):
                wfull = wg[:, s * heads_per_vreg:s * heads_per_vreg + 1]
                for hh in range(1, heads_per_vreg):
                    col = s * heads_per_vreg + hh
                    wfull = jnp.where(lane < hh * A_HEAD_DIM, wfull, wg[:, col:col + 1])
                term = wfull * o_g
                acc = term if acc is None else acc + term
            pieces.append(acc)

        cw = glu_ref.shape[1]
        conv = jnp.concatenate(
            [_conv_block(xpad, dww_ref, blk * rb, rb, slice(s * LANES, (s + 1) * LANES), halo)
             for s in range(cw // LANES)], axis=1) + dwb_ref[...]
        mu = jnp.mean(conv, axis=-1, keepdims=True)
        xc = conv - mu
        y = xc * lax.rsqrt(jnp.mean(xc * xc, axis=-1, keepdims=True) + NORM_EPS)
        c = _silu(y * lng_ref[...] + lnb_ref[...])
        cat[rows, :] = jnp.concatenate(pieces + [c], axis=1).astype(BF16)

        if (blk + 1) % (sub // rb) == 0:
            srows = slice((blk + 1) * rb - sub, (blk + 1) * rb)
            out_ref[srows, :] = h_ref[srows, :] + _dot(cat[srows, :], wout_ref[...])


def _hyb_out(h, os_, lses, glu, dw_w, dw_b, ln_g, ln_b, w_out, *, tm, seq):
    t, d = h.shape
    cc = glu.shape[1]
    halo = 4 * SUBLANES
    assert DILATIONS[0] == 1 and halo >= SUBLANES * -(-CONV_WIDTH // SUBLANES)
    row = lambda width: pl.BlockSpec((tm, width), lambda i: (i, 0))
    dil_row = lambda width: [pl.BlockSpec((tm // dil, dil * width), lambda i: (i, 0))
                             for dil in DILATIONS]
    prev = pl.BlockSpec((halo, cc), lambda i: (jnp.maximum(i * (tm // halo) - 1, 0), 0))
    body = functools.partial(_hyb_out_body, tm=tm, halo=halo, tiles_per_seq=seq // tm,
                             rb=8 * SUBLANES, sub=min(tm, MXU_DIM))
    nd = len(DILATIONS) - 1
    return pl.pallas_call(
        body,
        grid=(t // tm,),
        in_specs=[row(d)] + dil_row(A_WIDTH) + dil_row(LANES) + [row(cc), prev,
                  _resident(dw_w.shape), _resident((1, cc)), _resident((1, cc)),
                  _resident((1, cc)), _resident(w_out.shape)],
        out_specs=row(d),
        out_shape=jax.ShapeDtypeStruct((t, d), F32),
        scratch_shapes=[pltpu.VMEM((tm + halo, cc), F32),
                        pltpu.VMEM((nd * A_WIDTH // LANES, tm, LANES), F32),
                        pltpu.VMEM((nd, tm, LANES), F32),
                        pltpu.VMEM((tm, d), BF16)],
        compiler_params=_params(1),
        name="hyb_out",
    )(h, *os_, *lses, glu, glu, dw_w, dw_b, ln_g, ln_b, w_out)


def _gdn_in_body(h_ref, g_ref, w_ref, ws_ref, cw_ref, alog_ref, dtb_ref,
                 qkv_ref, z_ref, bg_ref, xn_ref, carry, *, tm, tiles_per_seq, chunk, rb):
    i = pl.program_id(0)
    xn_ref[...] = _rms_bf16(h_ref[...], g_ref[...])
    hw = GDN_HEADS * GDN_DIM
    halo = SUBLANES
    taps = GDN_SHORT_CONV

    @pl.when(i % tiles_per_seq == 0)
    def _():
        carry[...] = jnp.zeros_like(carry)

    def l2n(x):
        return x * lax.rsqrt(jnp.sum(x * x, axis=-1, keepdims=True) + NORM_EPS)

    def finish_qkv(c0, pc):
        for s0 in range(0, chunk, GDN_DIM):
            lanes = slice(c0 + s0, c0 + s0 + GDN_DIM)
            w0, w1, w2, w3 = (cw_ref[j:j + 1, lanes] for j in range(taps))
            for r0 in range(0, tm, rb):
                top = carry[:, lanes] if r0 == 0 else pc[r0 - halo:r0, s0:s0 + GDN_DIM]
                ext = jnp.concatenate([top, pc[r0:r0 + rb, s0:s0 + GDN_DIM]], axis=0)
                x1 = pltpu.roll(ext, 1, 0)
                even = ext * w3 + x1 * w2
                odd = ext * w1 + x1 * w0
                y = _silu((even + pltpu.roll(odd, 2, 0))[halo:])
                qkv_ref[r0:r0 + rb, lanes] = l2n(y) if c0 < 2 * hw else y
            carry[:, lanes] = pc[tm - halo:, s0:s0 + GDN_DIM]

    qkv_chunks = list(range(0, 3 * hw, chunk))
    z_chunks = list(range(3 * hw, 4 * hw, chunk))
    every = -(-len(qkv_chunks) // max(len(z_chunks), 1))
    order = []
    for n, c0 in enumerate(qkv_chunks):
        order.append(c0)
        if (n + 1) % every == 0 and z_chunks:
            order.append(z_chunks.pop(0))
    order += z_chunks
    pending = None
    for c0 in order + [None]:
        nxt = None if c0 is None else _dot(xn_ref[...], w_ref[:, c0:c0 + chunk])
        if pending is not None:
            p0, pc = pending
            if p0 < 3 * hw:
                finish_qkv(p0, pc)
            else:
                z_ref[:, p0 - 3 * hw:p0 - 3 * hw + chunk] = pc
        pending = (c0, nxt)

    small = _dot(xn_ref[...], ws_ref[...])
    x = small + dtb_ref[...]
    softplus = jnp.maximum(x, 0.0) + jnp.log1p(jnp.exp(-jnp.abs(x)))
    lane = lax.broadcasted_iota(jnp.int32, (1, LANES), 1)
    bg_ref[...] = jnp.where(lane < GDN_HEADS, jax.nn.sigmoid(small),
                            -jnp.exp(alog_ref[...]) * softplus)


def _gdn_in(h, g, w_main, w_small, conv_w, alog, dtb, *, tm, seq):
    t, d = h.shape
    hw = GDN_HEADS * GDN_DIM
    row = lambda width: pl.BlockSpec((tm, width), lambda i: (i, 0))
    body = functools.partial(_gdn_in_body, tm=tm, tiles_per_seq=seq // tm, chunk=2 * MXU_DIM,
                             rb=8 * SUBLANES)
    return pl.pallas_call(
        body,
        grid=(t // tm,),
        in_specs=[row(d), _resident((1, d)),
                  _resident(w_main.shape), _resident(w_small.shape), _resident(conv_w.shape),
                  _resident((1, LANES)), _resident((1, LANES))],
        out_specs=[row(3 * hw), row(hw), row(LANES)],
        out_shape=[jax.ShapeDtypeStruct((t, 3 * hw), F32), jax.ShapeDtypeStruct((t, hw), F32),
                   jax.ShapeDtypeStruct((t, LANES), F32)],
        scratch_shapes=[pltpu.VMEM((tm, d), BF16), pltpu.VMEM((SUBLANES, 3 * hw), F32)],
        compiler_params=_params(1),
        name="gdn_in",
    )(h, g, w_main, w_small, conv_w, alog, dtb)


def _split2(a):
    hi = a.astype(BF16)
    lo = (a - hi.astype(F32)).astype(BF16)
    return hi, lo


def _split_dot(a_parts, b_parts):
    a_hi, a_lo = a_parts
    b_hi, b_lo = b_parts
    return _dot(jnp.concatenate([a_hi, a_hi, a_lo], axis=1),
                jnp.concatenate([b_hi, b_lo, b_hi], axis=0))


def _unit_lower_inverses(lms, eye, nilpotency):
    n = lms[0].shape[0]
    pows = [-lm for lm in lms]
    sums = [eye + p for p in pows]
    pows = [_split_dot(ps, ps) for ps in (_split2(p) for p in pows)]
    m = 2
    while 2 * m < nilpotency:
        prods = []
        for p, s in zip(pows, sums):
            (p_hi, p_lo), (s_hi, s_lo) = _split2(p), _split2(s)
            wide = (jnp.concatenate([p_hi, s_hi], axis=1), jnp.concatenate([p_lo, s_lo], axis=1))
            prods.append(_split_dot((p_hi, p_lo), wide))
        sums = [s + r[:, n:] for s, r in zip(sums, prods)]
        pows = [r[:, :n] for r in prods]
        m *= 2
    return [s + _split_dot(_split2(p), _split2(s)) for s, p in zip(sums, pows)]


def _block_diag(a, b):
    zero = jnp.zeros_like(a)
    return jnp.concatenate([jnp.concatenate([a, zero], axis=1),
                            jnp.concatenate([zero, b], axis=1)], axis=0)


def _gdn_core_body(qkv_ref, z_ref, bg_ref, ng_ref, o_ref, state, *, tc):
    n = pl.program_id(1)
    c = GDN_CHUNK
    pr = 2 * c
    hw = GDN_HEADS * GDN_DIM
    dim = GDN_DIM
    scale = GDN_DIM ** -0.5
    head_pairs = GDN_HEADS // 2

    @pl.when(n == 0)
    def _():
        state[...] = jnp.zeros_like(state)

    ri = lax.broadcasted_iota(jnp.int32, (tc, tc), 0)
    ci = lax.broadcasted_iota(jnp.int32, (tc, tc), 1)
    tri_l = jnp.logical_and(ri // c == ci // c, ri >= ci).astype(BF16)
    bg = bg_ref[...]
    g1 = bg.astype(BF16)
    r1 = bg - g1.astype(F32)
    g2 = r1.astype(BF16)
    g3 = (r1 - g2.astype(F32)).astype(BF16)
    gc = _dot(jnp.concatenate([tri_l] * 3, axis=1), jnp.concatenate([g1, g2, g3], axis=0))
    gct = gc.T

    pi = lax.broadcasted_iota(jnp.int32, (pr, pr), 0)
    pj = lax.broadcasted_iota(jnp.int32, (pr, pr), 1)
    same_chunk = pi // c == pj // c
    eye_f = (pi == pj).astype(F32)
    causal = jnp.logical_and(same_chunk, pi >= pj)
    strict = jnp.logical_and(same_chunk, pi > pj)
    row_chunk = lax.broadcasted_iota(jnp.int32, (pr, 1), 0) // c

    for pp in range(tc // pr):
        lms, attns, rhs2s, q_decs, kd_ts = [], [], [], [], []
        rows = slice(pp * pr, (pp + 1) * pr)
        for hp in range(head_pairs):
            lhs1, ks, decays = [], [], []
            for hh in (2 * hp, 2 * hp + 1):
                q = qkv_ref[rows, hh * dim:(hh + 1) * dim] * scale
                k = qkv_ref[rows, hw + hh * dim:hw + (hh + 1) * dim]
                v = qkv_ref[rows, 2 * hw + hh * dim:2 * hw + (hh + 1) * dim]
                b_col = bg[rows, hh:hh + 1]
                g_col = gc[rows, GDN_HEADS + hh:GDN_HEADS + hh + 1]
                g_row = gct[GDN_HEADS + hh:GDN_HEADS + hh + 1, rows]
                decays.append(jnp.exp(jnp.where(causal, g_col - g_row, MASK_VALUE)))
                eg = jnp.exp(g_col)
                kb = k * b_col
                lhs1.append(jnp.concatenate([kb, q], axis=0).astype(BF16))
                ks.append(k.astype(BF16))
                rhs2s.append(jnp.concatenate([v * b_col, kb * eg], axis=1).astype(BF16))
                q_decs.append((q * eg).astype(BF16))
                kds = []
                for cc in range(2):
                    last = pp * pr + cc * c + c - 1
                    g_last = gc[last:last + 1, GDN_HEADS + hh:GDN_HEADS + hh + 1]
                    kd = jnp.where(row_chunk == cc, k * jnp.exp(g_last - g_col), 0.0)
                    kds.append((kd.T.astype(BF16), jnp.exp(g_last)))
                kd_ts.append(kds)
            p1 = _dot_nt(jnp.concatenate(lhs1, axis=1), _block_diag(ks[0], ks[1]))
            for e in range(2):
                blk = p1[:, e * pr:(e + 1) * pr]
                lms.append(jnp.where(strict, blk[:pr] * decays[e], 0.0))
                attns.append((blk[pr:] * decays[e]).astype(BF16))

        t_invs = _unit_lower_inverses(lms, eye_f, c)
        p2s = [_dot(t.astype(BF16), r) for t, r in zip(t_invs, rhs2s)]

        v_done = {hh: [] for hh in range(GDN_HEADS)}
        for cc in range(2):
            cr = slice(cc * c, (cc + 1) * c)
            orow = slice(pp * pr + cc * c, pp * pr + (cc + 1) * c)
            s_olds = [state[hh] for hh in range(GDN_HEADS)]
            p3s = []
            for hp in range(head_pairs):
                lhs = []
                for hh in (2 * hp, 2 * hp + 1):
                    unit = hh
                    w = p2s[unit][cr, dim:]
                    lhs.append(jnp.concatenate([w.astype(BF16), q_decs[unit][cr]], axis=0))
                s_bd = _block_diag(s_olds[2 * hp].astype(BF16), s_olds[2 * hp + 1].astype(BF16))
                p3s.append(_dot(jnp.concatenate(lhs, axis=1), s_bd))
            for hp in range(head_pairs):
                lhs, v_pairs = [], []
                for e, hh in enumerate((2 * hp, 2 * hp + 1)):
                    unit = hh
                    u = p2s[unit][:, :dim]
                    v_new = u[cr] - p3s[hp][:c, e * dim:(e + 1) * dim]
                    v_done[hh].append(v_new)
                    v_pairs.append(jnp.concatenate(v_done[hh] + [u[c:]] * (1 - cc),
                                                   axis=0).astype(BF16))
                    lhs.append(jnp.concatenate([attns[unit][cr], kd_ts[unit][cc][0]], axis=0))
                p45 = _dot(jnp.concatenate(lhs, axis=1), _block_diag(v_pairs[0], v_pairs[1]))
                for e, hh in enumerate((2 * hp, 2 * hp + 1)):
                    unit = hh
                    lanes = slice(e * dim, (e + 1) * dim)
                    state[hh] = s_olds[hh] * kd_ts[unit][cc][1] + p45[c:, lanes]
                    o = p3s[hp][c:, lanes] + p45[:c, lanes]
                    on = (o * lax.rsqrt(jnp.mean(o * o, axis=-1, keepdims=True) + NORM_EPS)
                          * ng_ref[...])
                    cols = slice(hh * dim, (hh + 1) * dim)
                    o_ref[orow, cols] = (on * _silu(z_ref[orow, cols])).astype(BF16)


def _gdn_core(qkv, z, bg, norm_g, *, batch, seq, tc):
    t, hw = z.shape
    tiles = seq // tc
    blk = lambda width: pl.BlockSpec((tc, width), lambda b, n: (b * tiles + n, 0))
    return pl.pallas_call(
        functools.partial(_gdn_core_body, tc=tc),
        grid=(batch, tiles),
        in_specs=[blk(3 * hw), blk(hw), blk(LANES), _resident((1, GDN_DIM))],
        out_specs=blk(hw),
        out_shape=jax.ShapeDtypeStruct((t, hw), BF16),
        scratch_shapes=[pltpu.VMEM((GDN_HEADS, GDN_DIM, GDN_DIM), F32)],
        compiler_params=_params(2),
        name="gdn_core",
    )(qkv, z, bg, norm_g)


def _tile(n, want):
    t = min(want, n)
    assert n % t == 0, (n, t)
    return t


def kernel(x, positions, ffn1_norm, ffn1_w_in, ffn1_w_out, mix_norm, ffn2_norm, ffn2_w_in,
           ffn2_w_out, hyb_w_in, hyb_dw_w, hyb_dw_b, hyb_ln_g, hyb_ln_b, hyb_w_out,
           gdn_w_in, gdn_conv_w, gdn_A_log, gdn_dt_bias, gdn_norm_g, gdn_w_out, final_norm):
    batch, seq, d = x.shape
    depth = ffn1_norm.shape[0]
    t = batch * seq
    h = x.reshape(t, d)
    pos = positions.reshape(t, 1)
    row = lambda a: a.reshape(1, -1).astype(F32)

    lane = jnp.arange(LANES) % A_HEAD_DIM
    inv_freq = jnp.power(jnp.float32(ROPE_THETA),
                         -jnp.arange(0, ROT_DIM, 2, dtype=F32) / ROT_DIM)
    half = ROT_DIM // 2
    invf = jnp.where(lane < ROT_DIM, inv_freq[lane % half], 0.0).astype(F32).reshape(1, LANES)
    shi = jnp.where((lane >= half) & (lane < ROT_DIM), 1.0, 0.0).astype(F32).reshape(1, LANES)
    slo = jnp.where(lane < half, -1.0, 0.0).astype(F32).reshape(1, LANES)
    bias = _attn_bias()

    gdn_hw = GDN_HEADS * GDN_DIM
    tm_ffn = _tile(t, 1024)
    tm_mix = _tile(seq, 512)
    final_g = row(final_norm)

    for layer in range(depth):
        i = layer // 2
        h = _ffn(h, row(ffn1_norm[layer]), ffn1_w_in[layer].astype(BF16),
                 ffn1_w_out[layer].astype(BF16), final_g, final_norm=False, tm=tm_ffn)
        if layer % 2 == 0:
            qkvs, glu = _hyb_in(h, row(mix_norm[layer]), hyb_w_in[i].astype(BF16), pos,
                                invf, shi, slo, tm=tm_mix)
            outs, lses = [], []
            for dil, (q, k, v) in zip(DILATIONS, qkvs):
                o_g, lse_g = _attn_branch(q, k, v, bias, dil, batch)
                outs.append(o_g)
                lses.append(lse_g)
            h = _hyb_out(h, outs, lses, glu, hyb_dw_w[i], row(hyb_dw_b[i]), row(hyb_ln_g[i]),
                         row(hyb_ln_b[i]), hyb_w_out[i].astype(BF16), tm=tm_mix, seq=seq)
            mix = None
        else:
            w_in = gdn_w_in[i]
            w_main = w_in[:, :4 * gdn_hw].astype(BF16)
            w_small = jnp.pad(w_in[:, 4 * gdn_hw:],
                              ((0, 0), (0, LANES - 2 * GDN_HEADS))).astype(BF16)
            lane_pad = lambda a: jnp.pad(a.astype(F32), (GDN_HEADS, LANES - 2 * GDN_HEADS)
                                         ).reshape(1, LANES)
            qkv, z, bg = _gdn_in(h, row(mix_norm[layer]), w_main, w_small, gdn_conv_w[i],
                                 lane_pad(gdn_A_log[i]), lane_pad(gdn_dt_bias[i]),
                                 tm=tm_mix, seq=seq)
            o = _gdn_core(qkv, z, bg, row(gdn_norm_g[i]), batch=batch, seq=seq,
                          tc=_tile(seq, 4 * GDN_CHUNK))
            mix = (o, gdn_w_out[i].astype(BF16))
        h = _ffn(h, row(ffn2_norm[layer]), ffn2_w_in[layer].astype(BF16),
                 ffn2_w_out[layer].astype(BF16), final_g,
                 final_norm=(layer == depth - 1), tm=tm_ffn, mix=mix)
    return h.reshape(batch, seq, d)
```

```python
import functools

import jax
import jax.numpy as jnp
from jax import lax
from jax.experimental import pallas as pl
from jax.experimental.pallas import tpu as pltpu

F32 = jnp.float32
BF16 = jnp.bfloat16

NORM_EPS = 1e-6
ROPE_THETA = 500000.0
A_HEADS = 8
A_HEAD_DIM = 64
A_WIDTH = A_HEADS * A_HEAD_DIM
ROT_DIM = A_HEAD_DIM // 4
DILATIONS = (1, 4, 16)
WIN_BLOCK = 128
CONV_WIDTH = 31
GDN_HEADS = 8
GDN_DIM = 128
GDN_CHUNK = 64
GDN_SHORT_CONV = 4

LANES = 128
SUBLANES = 8
MXU_DIM = 256
VMEM_LIMIT = 56 * 1024 * 1024
MASK_VALUE = -1e30
LOG2_E = 1.4426950408889634


def _params(n_axes):
    return pltpu.CompilerParams(dimension_semantics=("arbitrary",) * n_axes,
                                vmem_limit_bytes=VMEM_LIMIT)


def _resident(shape):
    nd = len(shape)
    return pl.BlockSpec(shape, lambda *_: (0,) * nd, pipeline_mode=pl.Buffered(1))


def _rms_bf16(x, g):
    ms = jnp.mean(x * x, axis=-1, keepdims=True)
    return (x * lax.rsqrt(ms + NORM_EPS) * g).astype(BF16)


def _silu(x):
    return x * jax.nn.sigmoid(x)


def _dot(a, b):
    return jnp.dot(a, b, preferred_element_type=F32)


def _dot_nt(a, b):
    return lax.dot_general(a, b, (((1,), (1,)), ((), ())), preferred_element_type=F32)


def _dot_tn(a, b):
    return lax.dot_general(a, b, (((0,), (0,)), ((), ())), preferred_element_type=F32)


def _ffn_body(h_ref, g_ref, win_ref, wout_ref, fg_ref, *refs, ffn, chunks, final_norm, mixed):
    if mixed:
        mix_ref, wmix_ref, o_ref, xn_ref, acc_ref, hres_ref = refs
        hres_ref[...] = h_ref[...] + _dot(mix_ref[...], wmix_ref[...])
        h_ref = hres_ref
    else:
        o_ref, xn_ref, acc_ref = refs
    xn_ref[...] = _rms_bf16(h_ref[...], g_ref[...])
    for n, (c0, cw) in enumerate(chunks):
        xn = xn_ref[...]
        gate = _dot(xn, win_ref[:, c0:c0 + cw])
        up = _dot(xn, win_ref[:, ffn + c0:ffn + c0 + cw])
        part = _dot((_silu(gate) * up).astype(BF16), wout_ref[c0:c0 + cw, :])
        if n == 0:
            acc_ref[...] = part
        else:
            acc_ref[...] += part
    out = h_ref[...] + 0.5 * acc_ref[...]
    if final_norm:
        ms = jnp.mean(out * out, axis=-1, keepdims=True)
        out = out * lax.rsqrt(ms + NORM_EPS) * fg_ref[...]
    o_ref[...] = out


def _ffn_chunks(ffn, width):
    out, c0 = [], 0
    while c0 < ffn:
        cw = min(width, ffn - c0)
        out.append((c0, cw))
        c0 += cw
    return tuple(out)


def _ffn(h, g, w_in, w_out, final_g, *, final_norm, tm, mix=None):
    t, d = h.shape
    ffn = w_out.shape[0]
    body = functools.partial(_ffn_body, ffn=ffn, chunks=_ffn_chunks(ffn, 4 * MXU_DIM),
                             final_norm=final_norm, mixed=mix is not None)
    row = lambda width: pl.BlockSpec((tm, width), lambda i: (i, 0))
    in_specs = [row(d), _resident((1, d)), _resident(w_in.shape), _resident(w_out.shape),
                _resident((1, d))]
    scratch = [pltpu.VMEM((tm, d), BF16), pltpu.VMEM((tm, d), F32)]
    args = [h, g, w_in, w_out, final_g]
    if mix is not None:
        in_specs += [row(mix[0].shape[1]), _resident(mix[1].shape)]
        scratch.append(pltpu.VMEM((tm, d), F32))
        args += list(mix)
    return pl.pallas_call(
        body,
        grid=(t // tm,),
        in_specs=in_specs,
        out_specs=row(d),
        out_shape=jax.ShapeDtypeStruct((t, d), F32),
        scratch_shapes=scratch,
        compiler_params=_params(1),
        name="ffn_final" if final_norm else ("ffn_mix" if mix is not None else "ffn"),
    )(*args)


def _hyb_in_body(h_ref, g_ref, w_ref, pos_ref, invf_ref, shi_ref, slo_ref, *refs, tm):
    nd = len(DILATIONS)
    qkv_refs = [refs[3 * n:3 * n + 3] for n in range(nd)]
    glu_ref, stage = refs[3 * nd], refs[3 * nd + 1]
    xn = _rms_bf16(h_ref[...], g_ref[...])
    proj = _dot(xn, w_ref[...])
    ang = pos_ref[...].astype(F32) * invf_ref[...]
    cos, sin = jnp.cos(ang), jnp.sin(ang)
    reps = A_WIDTH // LANES
    cos4 = jnp.concatenate([cos] * reps, axis=1)
    sin_hi = jnp.concatenate([sin * shi_ref[...]] * reps, axis=1)
    sin_lo = jnp.concatenate([sin * slo_ref[...]] * reps, axis=1)
    half = ROT_DIM // 2

    def rot(x):
        return (x * cos4 + pltpu.roll(x, half, 1) * sin_hi
                + pltpu.roll(x, A_WIDTH - half, 1) * sin_lo)

    qkv = (rot(proj[:, :A_WIDTH]) * (A_HEAD_DIM ** -0.5 * LOG2_E),
           rot(proj[:, A_WIDTH:2 * A_WIDTH]),
           proj[:, 2 * A_WIDTH:3 * A_WIDTH])
    u = proj[:, 3 * A_WIDTH:]
    c = u.shape[1] // 2
    glu_ref[...] = u[:, :c] * jax.nn.sigmoid(u[:, c:])

    slabs = A_WIDTH // LANES
    for a, x in enumerate(qkv):
        for s in range(slabs):
            stage[a * slabs + s] = x[:, s * LANES:(s + 1) * LANES]
    for dil, out_refs in zip(DILATIONS, qkv_refs):
        for a, out_ref in enumerate(out_refs):
            if dil == 1:
                out_ref[...] = qkv[a].astype(BF16)
                continue
            for cls in range(dil):
                for s in range(slabs):
                    piece = stage[a * slabs + s, pl.ds(cls, tm // dil, stride=dil), :]
                    lane0 = cls * A_WIDTH + s * LANES
                    out_ref[:, lane0:lane0 + LANES] = piece.astype(BF16)


def _hyb_in(h, g, w, pos, invf, shi, slo, *, tm):
    t, d = h.shape
    cc = (w.shape[1] - 3 * A_WIDTH) // 2
    row = lambda width: pl.BlockSpec((tm, width), lambda i: (i, 0))
    dil_specs, dil_shapes = [], []
    for dil in DILATIONS:
        dil_specs += [pl.BlockSpec((tm // dil, dil * A_WIDTH), lambda i: (i, 0))] * 3
        dil_shapes += [jax.ShapeDtypeStruct((t // dil, dil * A_WIDTH), BF16)] * 3
    outs = pl.pallas_call(
        functools.partial(_hyb_in_body, tm=tm),
        grid=(t // tm,),
        in_specs=[row(d), _resident((1, d)), _resident(w.shape), row(1),
                  _resident((1, LANES)), _resident((1, LANES)), _resident((1, LANES))],
        out_specs=dil_specs + [row(cc)],
        out_shape=dil_shapes + [jax.ShapeDtypeStruct((t, cc), F32)],
        scratch_shapes=[pltpu.VMEM((3 * A_WIDTH // LANES, tm, LANES), F32)],
        compiler_params=_params(1),
        name="hyb_in",
    )(h, g, w, pos, invf, shi, slo)
    n = len(DILATIONS)
    return [tuple(outs[3 * i:3 * i + 3]) for i in range(n)], outs[3 * n]


def _attn_body(q_ref, kc_ref, kp_ref, vc_ref, vp_ref, bias_ref, o_ref, lse_ref, kcat, vcat, *,
               tq, nc):
    i = pl.program_id(2)
    w = WIN_BLOCK
    kcat[0:w, :] = kp_ref[0]
    kcat[w:, :] = kc_ref[0]
    vcat[0:w, :] = vp_ref[0]
    vcat[w:, :] = vc_ref[0]
    slab = MXU_DIM
    hps = slab // A_HEAD_DIM
    lane_head = lax.broadcasted_iota(jnp.int32, (1, slab), 1) // A_HEAD_DIM
    lane_idx = lax.broadcasted_iota(jnp.int32, (1, LANES), 1)
    for cls in range(nc):
        for j in range(tq // w):
            first = jnp.where(jnp.logical_and(i == 0, j == 0), 1, 0) if j == 0 else 0
            bias = bias_ref[first]
            lse_tile = jnp.zeros((w, LANES), F32)
            for s in range(A_WIDTH // slab):
                cols = slice(cls * A_WIDTH + s * slab, cls * A_WIDTH + (s + 1) * slab)
                qs = q_ref[0, j * w:(j + 1) * w, cols]
                lhs = jnp.concatenate(
                    [jnp.where(lane_head == hh, qs, jnp.zeros_like(qs)) for hh in range(hps)],
                    axis=0)
                sc = _dot_nt(lhs, kcat[j * w:(j + 2) * w, cols]) + bias
                m = jnp.max(sc, axis=1, keepdims=True)
                p = jnp.exp2(sc - m)
                den = jnp.sum(p, axis=1, keepdims=True)
                pv = _dot(p.astype(BF16), vcat[j * w:(j + 2) * w, cols]) * (1.0 / den)
                lse = m + jnp.log2(den)
                o_slab = jnp.zeros((w, slab), F32)
                for hh in range(hps):
                    rows = slice(hh * w, (hh + 1) * w)
                    o_slab = o_slab + jnp.where(lane_head == hh, pv[rows], 0.0)
                    lse_tile = lse_tile + jnp.where(lane_idx == s * hps + hh, lse[rows], 0.0)
                o_ref[0, j * w:(j + 1) * w, cols] = o_slab
            lse_ref[0, j * w:(j + 1) * w, cls * LANES:(cls + 1) * LANES] = lse_tile


def _attn_bias():
    w = WIN_BLOCK
    r = jnp.arange(w)[:, None]
    c = jnp.arange(2 * w)[None, :]
    band = (c >= r) & (c <= r + w)
    no_prev = band & (c >= w)
    bias = jnp.where(jnp.stack([band, no_prev]), 0.0, MASK_VALUE).astype(F32)
    return jnp.tile(bias, (1, MXU_DIM // A_HEAD_DIM, 1))


def _attn_branch(q, k, v, bias, dil, batch):
    width = A_WIDTH
    b = batch
    l = q.shape[0] // b
    w = WIN_BLOCK
    rows_per_step = 4 * w
    tq = min(rows_per_step, l)
    nc = min(dil, rows_per_step // tq)
    view = lambda a: a.reshape(b, l, dil * width)
    cur = pl.BlockSpec((1, tq, nc * width), lambda bi, c, i: (bi, i, c))
    prev = pl.BlockSpec((1, w, nc * width),
                        lambda bi, c, i: (bi, jnp.maximum(i * (tq // w) - 1, 0), c))
    o, lse = pl.pallas_call(
        functools.partial(_attn_body, tq=tq, nc=nc),
        grid=(b, dil // nc, l // tq),
        in_specs=[cur, cur, prev, cur, prev, _resident(bias.shape)],
        out_specs=[cur, pl.BlockSpec((1, tq, nc * LANES), lambda bi, c, i: (bi, i, c))],
        out_shape=[jax.ShapeDtypeStruct((b, l, dil * width), F32),
                   jax.ShapeDtypeStruct((b, l, dil * LANES), F32)],
        scratch_shapes=[pltpu.VMEM((tq + w, nc * width), BF16),
                        pltpu.VMEM((tq + w, nc * width), BF16)],
        compiler_params=_params(3),
        name=f"attn_d{dil}",
    )(view(q), view(k), view(k), view(v), view(v), bias)
    return o.reshape(b * l, dil * width), lse.reshape(b * l, dil * LANES)


def _conv_block(xpad, dww_ref, r0, nrows, lanes, halo):
    base = r0 + halo - SUBLANES
    acc = None
    for b in range(SUBLANES):
        z = None
        for a in range(-(-CONV_WIDTH // SUBLANES)):
            delay = SUBLANES * a + b
            if delay >= CONV_WIDTH:
                continue
            j = CONV_WIDTH - 1 - delay
            start = base - SUBLANES * a
            term = xpad[start:start + nrows + SUBLANES, lanes] * dww_ref[j:j + 1, lanes]
            z = term if z is None else z + term
        zs = z if b == 0 else pltpu.roll(z, b, 0)
        acc = zs if acc is None else acc + zs
    return acc[SUBLANES:]


def _hyb_out_body(h_ref, o1_ref, o2_ref, o3_ref, l1_ref, l2_ref, l3_ref, glu_ref, gprev_ref,
                  dww_ref, dwb_ref, lng_ref, lnb_ref, wout_ref, out_ref,
                  xpad, ostage, lstage, cat, *, tm, halo, tiles_per_seq, rb, sub):
    i = pl.program_id(0)
    slabs = A_WIDTH // LANES
    for n, (dil, o_ref, l_ref) in enumerate(zip(DILATIONS[1:], (o2_ref, o3_ref), (l2_ref, l3_ref))):
        for cls in range(dil):
            rows = pl.ds(cls, tm // dil, stride=dil)
            for s in range(slabs):
                lane0 = cls * A_WIDTH + s * LANES
                ostage[n * slabs + s, rows, :] = o_ref[:, lane0:lane0 + LANES]
            lstage[n, rows, :] = l_ref[:, cls * LANES:(cls + 1) * LANES]

    prev = gprev_ref[...]
    xpad[0:halo, :] = jnp.where(i % tiles_per_seq == 0, jnp.zeros_like(prev), prev)
    xpad[halo:, :] = glu_ref[...]

    lane = lax.broadcasted_iota(jnp.int32, (1, LANES), 1)
    heads_per_vreg = LANES // A_HEAD_DIM
    for blk in range(tm // rb):
        rows = slice(blk * rb, (blk + 1) * rb)
        l1, l2, l3 = l1_ref[rows, :], lstage[0, rows, :], lstage[1, rows, :]
        mx = jnp.maximum(jnp.maximum(l1, l2), l3)
        e1, e2, e3 = jnp.exp2(l1 - mx), jnp.exp2(l2 - mx), jnp.exp2(l3 - mx)
        tot = e1 + e2 + e3
        inv_tot = 1.0 / tot
        ws = (e1 * inv_tot, e2 * inv_tot, e3 * inv_tot)
        pieces = []
        for s in range(slabs):
            o_gs = (o1_ref[rows, s * LANES:(s + 1) * LANES], ostage[s, rows, :],
                    ostage[slabs + s, rows, :])
            acc = None
            for wg, o_g in zip(ws, o_gs):
                wfull = wg[:, s * heads_per_vreg:s * heads_per_vreg + 1]
                for hh in range(1, heads_per_vreg):
                    col = s * heads_per_vreg + hh
                    wfull = jnp.where(lane < hh * A_HEAD_DIM, wfull, wg[:, col:col + 1])
                term = wfull * o_g
                acc = term if acc is None else acc + term
            pieces.append(acc)

        cw = glu_ref.shape[1]
        conv = jnp.concatenate(
            [_conv_block(xpad, dww_ref, blk * rb, rb, slice(s * LANES, (s + 1) * LANES), halo)
             for s in range(cw // LANES)], axis=1) + dwb_ref[...]
        mu = jnp.mean(conv, axis=-1, keepdims=True)
        xc = conv - mu
        y = xc * lax.rsqrt(jnp.mean(xc * xc, axis=-1, keepdims=True) + NORM_EPS)
        c = _silu(y * lng_ref[...] + lnb_ref[...])
        cat[rows, :] = jnp.concatenate(pieces + [c], axis=1).astype(BF16)

        if (blk + 1) % (sub // rb) == 0:
            srows = slice((blk + 1) * rb - sub, (blk + 1) * rb)
            out_ref[srows, :] = h_ref[srows, :] + _dot(cat[srows, :], wout_ref[...])


def _hyb_out(h, os_, lses, glu, dw_w, dw_b, ln_g, ln_b, w_out, *, tm, seq):
    t, d = h.shape
    cc = glu.shape[1]
    halo = 4 * SUBLANES
    assert DILATIONS[0] == 1 and halo >= SUBLANES * -(-CONV_WIDTH // SUBLANES)
    row = lambda width: pl.BlockSpec((tm, width), lambda i: (i, 0))
    dil_row = lambda width: [pl.BlockSpec((tm // dil, dil * width), lambda i: (i, 0))
                             for dil in DILATIONS]
    prev = pl.BlockSpec((halo, cc), lambda i: (jnp.maximum(i * (tm // halo) - 1, 0), 0))
    body = functools.partial(_hyb_out_body, tm=tm, halo=halo, tiles_per_seq=seq // tm,
                             rb=8 * SUBLANES, sub=min(tm, MXU_DIM))
    nd = len(DILATIONS) - 1
    return pl.pallas_call(
        body,
        grid=(t // tm,),
        in_specs=[row(d)] + dil_row(A_WIDTH) + dil_row(LANES) + [row(cc), prev,
                  _resident(dw_w.shape), _resident((1, cc)), _resident((1, cc)),
                  _resident((1, cc)), _resident(w_out.shape)],
        out_specs=row(d),
        out_shape=jax.ShapeDtypeStruct((t, d), F32),
        scratch_shapes=[pltpu.VMEM((tm + halo, cc), F32),
                        pltpu.VMEM((nd * A_WIDTH // LANES, tm, LANES), F32),
                        pltpu.VMEM((nd, tm, LANES), F32),
                        pltpu.VMEM((tm, d), BF16)],
        compiler_params=_params(1),
        name="hyb_out",
    )(h, *os_, *lses, glu, glu, dw_w, dw_b, ln_g, ln_b, w_out)


def _gdn_in_body(h_ref, g_ref, w_ref, ws_ref, cw_ref, alog_ref, dtb_ref,
                 qkv_ref, z_ref, bg_ref, xn_ref, carry, *, tm, tiles_per_seq, chunk, rb):
    i = pl.program_id(0)
    xn_ref[...] = _rms_bf16(h_ref[...], g_ref[...])
    hw = GDN_HEADS * GDN_DIM
    halo = SUBLANES
    taps = GDN_SHORT_CONV

    @pl.when(i % tiles_per_seq == 0)
    def _():
        carry[...] = jnp.zeros_like(carry)

    def l2n(x):
        return x * lax.rsqrt(jnp.sum(x * x, axis=-1, keepdims=True) + NORM_EPS)

    def finish_qkv(c0, pc):
        for s0 in range(0, chunk, GDN_DIM):
            lanes = slice(c0 + s0, c0 + s0 + GDN_DIM)
            w0, w1, w2, w3 = (cw_ref[j:j + 1, lanes] for j in range(taps))
            for r0 in range(0, tm, rb):
                top = carry[:, lanes] if r0 == 0 else pc[r0 - halo:r0, s0:s0 + GDN_DIM]
                ext = jnp.concatenate([top, pc[r0:r0 + rb, s0:s0 + GDN_DIM]], axis=0)
                x1 = pltpu.roll(ext, 1, 0)
                even = ext * w3 + x1 * w2
                odd = ext * w1 + x1 * w0
                y = _silu((even + pltpu.roll(odd, 2, 0))[halo:])
                qkv_ref[r0:r0 + rb, lanes] = l2n(y) if c0 < 2 * hw else y
            carry[:, lanes] = pc[tm - halo:, s0:s0 + GDN_DIM]

    qkv_chunks = list(range(0, 3 * hw, chunk))
    z_chunks = list(range(3 * hw, 4 * hw, chunk))
    every = -(-len(qkv_chunks) // max(len(z_chunks), 1))
    order = []
    for n, c0 in enumerate(qkv_chunks):
        order.append(c0)
        if (n + 1) % every == 0 and z_chunks:
            order.append(z_chunks.pop(0))
    order += z_chunks
    pending = None
    for c0 in order + [None]:
        nxt = None if c0 is None else _dot(xn_ref[...], w_ref[:, c0:c0 + chunk])
        if pending is not None:
            p0, pc = pending
            if p0 < 3 * hw:
                finish_qkv(p0, pc)
            else:
                z_ref[:, p0 - 3 * hw:p0 - 3 * hw + chunk] = pc
        pending = (c0, nxt)

    small = _dot(xn_ref[...], ws_ref[...])
    x = small + dtb_ref[...]
    softplus = jnp.maximum(x, 0.0) + jnp.log1p(jnp.exp(-jnp.abs(x)))
    lane = lax.broadcasted_iota(jnp.int32, (1, LANES), 1)
    bg_ref[...] = jnp.where(lane < GDN_HEADS, jax.nn.sigmoid(small),
                            -jnp.exp(alog_ref[...]) * softplus)


def _gdn_in(h, g, w_main, w_small, conv_w, alog, dtb, *, tm, seq):
    t, d = h.shape
    hw = GDN_HEADS * GDN_DIM
    row = lambda width: pl.BlockSpec((tm, width), lambda i: (i, 0))
    body = functools.partial(_gdn_in_body, tm=tm, tiles_per_seq=seq // tm, chunk=2 * MXU_DIM,
                             rb=8 * SUBLANES)
    return pl.pallas_call(
        body,
        grid=(t // tm,),
        in_specs=[row(d), _resident((1, d)),
                  _resident(w_main.shape), _resident(w_small.shape), _resident(conv_w.shape),
                  _resident((1, LANES)), _resident((1, LANES))],
        out_specs=[row(3 * hw), row(hw), row(LANES)],
        out_shape=[jax.ShapeDtypeStruct((t, 3 * hw), F32), jax.ShapeDtypeStruct((t, hw), F32),
                   jax.ShapeDtypeStruct((t, LANES), F32)],
        scratch_shapes=[pltpu.VMEM((tm, d), BF16), pltpu.VMEM((SUBLANES, 3 * hw), F32)],
        compiler_params=_params(1),
        name="gdn_in",
    )(h, g, w_main, w_small, conv_w, alog, dtb)


def _split2(a):
    hi = a.astype(BF16)
    lo = (a - hi.astype(F32)).astype(BF16)
    return hi, lo


def _split_dot(a_parts, b_parts):
    a_hi, a_lo = a_parts
    b_hi, b_lo = b_parts
    return _dot(jnp.concatenate([a_hi, a_hi, a_lo], axis=1),
                jnp.concatenate([b_hi, b_lo, b_hi], axis=0))


def _unit_lower_inverses(lms, eye, nilpotency):
    n = lms[0].shape[0]
    pows = [-lm for lm in lms]
    sums = [eye + p for p in pows]
    pows = [_split_dot(ps, ps) for ps in (_split2(p) for p in pows)]
    m = 2
    while 2 * m < nilpotency:
        prods = []
        for p, s in zip(pows, sums):
            (p_hi, p_lo), (s_hi, s_lo) = _split2(p), _split2(s)
            wide = (jnp.concatenate([p_hi, s_hi], axis=1), jnp.concatenate([p_lo, s_lo], axis=1))
            prods.append(_split_dot((p_hi, p_lo), wide))
        sums = [s + r[:, n:] for s, r in zip(sums, prods)]
        pows = [r[:, :n] for r in prods]
        m *= 2
    return [s + _split_dot(_split2(p), _split2(s)) for s, p in zip(sums, pows)]


def _block_diag(a, b):
    zero = jnp.zeros_like(a)
    return jnp.concatenate([jnp.concatenate([a, zero], axis=1),
                            jnp.concatenate([zero, b], axis=1)], axis=0)


def _gdn_core_body(qkv_ref, z_ref, bg_ref, ng_ref, o_ref, state, *, tc):
    n = pl.program_id(1)
    c = GDN_CHUNK
    pr = 2 * c
    hw = GDN_HEADS * GDN_DIM
    dim = GDN_DIM
    scale = GDN_DIM ** -0.5
    head_pairs = GDN_HEADS // 2

    @pl.when(n == 0)
    def _():
        state[...] = jnp.zeros_like(state)

    ri = lax.broadcasted_iota(jnp.int32, (tc, tc), 0)
    ci = lax.broadcasted_iota(jnp.int32, (tc, tc), 1)
    tri_l = jnp.logical_and(ri // c == ci // c, ri >= ci).astype(BF16)
    bg = bg_ref[...]
    g1 = bg.astype(BF16)
    r1 = bg - g1.astype(F32)
    g2 = r1.astype(BF16)
    g3 = (r1 - g2.astype(F32)).astype(BF16)
    gc = _dot(jnp.concatenate([tri_l] * 3, axis=1), jnp.concatenate([g1, g2, g3], axis=0))
    gct = gc.T

    pi = lax.broadcasted_iota(jnp.int32, (pr, pr), 0)
    pj = lax.broadcasted_iota(jnp.int32, (pr, pr), 1)
    same_chunk = pi // c == pj // c
    eye_f = (pi == pj).astype(F32)
    causal = jnp.logical_and(same_chunk, pi >= pj)
    strict = jnp.logical_and(same_chunk, pi > pj)
    row_chunk = lax.broadcasted_iota(jnp.int32, (pr, 1), 0) // c

    for pp in range(tc // pr):
        lms, attns, rhs2s, q_decs, kd_ts = [], [], [], [], []
        rows = slice(pp * pr, (pp + 1) * pr)
        for hp in range(head_pairs):
            lhs1, ks, decays = [], [], []
            for hh in (2 * hp, 2 * hp + 1):
                q = qkv_ref[rows, hh * dim:(hh + 1) * dim] * scale
                k = qkv_ref[rows, hw + hh * dim:hw + (hh + 1) * dim]
                v = qkv_ref[rows, 2 * hw + hh * dim:2 * hw + (hh + 1) * dim]
                b_col = bg[rows, hh:hh + 1]
                g_col = gc[rows, GDN_HEADS + hh:GDN_HEADS + hh + 1]
                g_row = gct[GDN_HEADS + hh:GDN_HEADS + hh + 1, rows]
                decays.append(jnp.exp(jnp.where(causal, g_col - g_row, MASK_VALUE)))
                eg = jnp.exp(g_col)
                kb = k * b_col
                lhs1.append(jnp.concatenate([kb, q], axis=0).astype(BF16))
                ks.append(k.astype(BF16))
                rhs2s.append(jnp.concatenate([v * b_col, kb * eg], axis=1).astype(BF16))
                q_decs.append((q * eg).astype(BF16))
                kds = []
                for cc in range(2):
                    last = pp * pr + cc * c + c - 1
                    g_last = gc[last:last + 1, GDN_HEADS + hh:GDN_HEADS + hh + 1]
                    kd = jnp.where(row_chunk == cc, k * jnp.exp(g_last - g_col), 0.0)
                    kds.append((kd.T.astype(BF16), jnp.exp(g_last)))
                kd_ts.append(kds)
            p1 = _dot_nt(jnp.concatenate(lhs1, axis=1), _block_diag(ks[0], ks[1]))
            for e in range(2):
                blk = p1[:, e * pr:(e + 1) * pr]
                lms.append(jnp.where(strict, blk[:pr] * decays[e], 0.0))
                attns.append((blk[pr:] * decays[e]).astype(BF16))

        t_invs = _unit_lower_inverses(lms, eye_f, c)
        p2s = [_dot(t.astype(BF16), r) for t, r in zip(t_invs, rhs2s)]

        v_done = {hh: [] for hh in range(GDN_HEADS)}
        for cc in range(2):
            cr = slice(cc * c, (cc + 1) * c)
            orow = slice(pp * pr + cc * c, pp * pr + (cc + 1) * c)
            s_olds = [state[hh] for hh in range(GDN_HEADS)]
            p3s = []
            for hp in range(head_pairs):
                lhs = []
                for hh in (2 * hp, 2 * hp + 1):
                    unit = hh
                    w = p2s[unit][cr, dim:]
                    lhs.append(jnp.concatenate([w.astype(BF16), q_decs[unit][cr]], axis=0))
                s_bd = _block_diag(s_olds[2 * hp].astype(BF16), s_olds[2 * hp + 1].astype(BF16))
                p3s.append(_dot(jnp.concatenate(lhs, axis=1), s_bd))
            for hp in range(head_pairs):
                lhs, v_pairs = [], []
                for e, hh in enumerate((2 * hp, 2 * hp + 1)):
                    unit = hh
                    u = p2s[unit][:, :dim]
                    v_new = u[cr] - p3s[hp][:c, e * dim:(e + 1) * dim]
                    v_done[hh].append(v_new)
                    v_pairs.append(jnp.concatenate(v_done[hh] + [u[c:]] * (1 - cc),
                                                   axis=0).astype(BF16))
                    lhs.append(jnp.concatenate([attns[unit][cr], kd_ts[unit][cc][0]], axis=0))
                p45 = _dot(jnp.concatenate(lhs, axis=1), _block_diag(v_pairs[0], v_pairs[1]))
                for e, hh in enumerate((2 * hp, 2 * hp + 1)):
                    unit = hh
                    lanes = slice(e * dim, (e + 1) * dim)
                    state[hh] = s_olds[hh] * kd_ts[unit][cc][1] + p45[c:, lanes]
                    o = p3s[hp][c:, lanes] + p45[:c, lanes]
                    on = (o * lax.rsqrt(jnp.mean(o * o, axis=-1, keepdims=True) + NORM_EPS)
                          * ng_ref[...])
                    cols = slice(hh * dim, (hh + 1) * dim)
                    o_ref[orow, cols] = (on * _silu(z_ref[orow, cols])).astype(BF16)


def _gdn_core(qkv, z, bg, norm_g, *, batch, seq, tc):
    t, hw = z.shape
    tiles = seq // tc
    blk = lambda width: pl.BlockSpec((tc, width), lambda b, n: (b * tiles + n, 0))
    return pl.pallas_call(
        functools.partial(_gdn_core_body, tc=tc),
        grid=(batch, tiles),
        in_specs=[blk(3 * hw), blk(hw), blk(LANES), _resident((1, GDN_DIM))],
        out_specs=blk(hw),
        out_shape=jax.ShapeDtypeStruct((t, hw), BF16),
        scratch_shapes=[pltpu.VMEM((GDN_HEADS, GDN_DIM, GDN_DIM), F32)],
        compiler_params=_params(2),
        name="gdn_core",
    )(qkv, z, bg, norm_g)


def _tile(n, want):
    t = min(want, n)
    assert n % t == 0, (n, t)
    return t


def kernel(x, positions, ffn1_norm, ffn1_w_in, ffn1_w_out, mix_norm, ffn2_norm, ffn2_w_in,
           ffn2_w_out, hyb_w_in, hyb_dw_w, hyb_dw_b, hyb_ln_g, hyb_ln_b, hyb_w_out,
           gdn_w_in, gdn_conv_w, gdn_A_log, gdn_dt_bias, gdn_norm_g, gdn_w_out, final_norm):
    batch, seq, d = x.shape
    depth = ffn1_norm.shape[0]
    t = batch * seq
    h = x.reshape(t, d)
    pos = positions.reshape(t, 1)
    row = lambda a: a.reshape(1, -1).astype(F32)

    lane = jnp.arange(LANES) % A_HEAD_DIM
    inv_freq = jnp.power(jnp.float32(ROPE_THETA),
                         -jnp.arange(0, ROT_DIM, 2, dtype=F32) / ROT_DIM)
    half = ROT_DIM // 2
    invf = jnp.where(lane < ROT_DIM, inv_freq[lane % half], 0.0).astype(F32).reshape(1, LANES)
    shi = jnp.where((lane >= half) & (lane < ROT_DIM), 1.0, 0.0).astype(F32).reshape(1, LANES)
    slo = jnp.where(lane < half, -1.0, 0.0).astype(F32).reshape(1, LANES)
    bias = _attn_bias()

    gdn_hw = GDN_HEADS * GDN_DIM
    tm_ffn = _tile(t, 1024)
    tm_mix = _tile(seq, 512)
    tm_hyb = _tile(seq, 1024)
    final_g = row(final_norm)

    for layer in range(depth):
        i = layer // 2
        h = _ffn(h, row(ffn1_norm[layer]), ffn1_w_in[layer].astype(BF16),
                 ffn1_w_out[layer].astype(BF16), final_g, final_norm=False, tm=tm_ffn)
        if layer % 2 == 0:
            qkvs, glu = _hyb_in(h, row(mix_norm[layer]), hyb_w_in[i].astype(BF16), pos,
                                invf, shi, slo, tm=tm_hyb)
            outs, lses = [], []
            for dil, (q, k, v) in zip(DILATIONS, qkvs):
                o_g, lse_g = _attn_branch(q, k, v, bias, dil, batch)
                outs.append(o_g)
                lses.append(lse_g)
            h = _hyb_out(h, outs, lses, glu, hyb_dw_w[i], row(hyb_dw_b[i]), row(hyb_ln_g[i]),
                         row(hyb_ln_b[i]), hyb_w_out[i].astype(BF16), tm=tm_mix, seq=seq)
            mix = None
        else:
            w_in = gdn_w_in[i]
            w_main = w_in[:, :4 * gdn_hw].astype(BF16)
            w_small = jnp.pad(w_in[:, 4 * gdn_hw:],
                              ((0, 0), (0, LANES - 2 * GDN_HEADS))).astype(BF16)
            lane_pad = lambda a: jnp.pad(a.astype(F32), (GDN_HEADS, LANES - 2 * GDN_HEADS)
                                         ).reshape(1, LANES)
            qkv, z, bg = _gdn_in(h, row(mix_norm[layer]), w_main, w_small, gdn_conv_w[i],
                                 lane_pad(gdn_A_log[i]), lane_pad(gdn_dt_bias[i]),
                                 tm=tm_mix, seq=seq)
            o = _gdn_core(qkv, z, bg, row(gdn_norm_g[i]), batch=batch, seq=seq,
                          tc=_tile(seq, 4 * GDN_CHUNK))
            mix = (o, gdn_w_out[i].astype(BF16))
        h = _ffn(h, row(ffn2_norm[layer]), ffn2_w_in[layer].astype(BF16),
                 ffn2_w_out[layer].astype(BF16), final_g,
                 final_norm=(layer == depth - 1), tm=tm_ffn, mix=mix)
    return h.reshape(batch, seq, d)
```

```python
import functools

import jax
import jax.numpy as jnp
from jax import lax
from jax.experimental import pallas as pl
from jax.experimental.pallas import tpu as pltpu

F32 = jnp.float32
BF16 = jnp.bfloat16

NORM_EPS = 1e-6
ROPE_THETA = 500000.0
A_HEADS = 8
A_HEAD_DIM = 64
A_WIDTH = A_HEADS * A_HEAD_DIM
ROT_DIM = A_HEAD_DIM // 4
DILATIONS = (1, 4, 16)
WIN_BLOCK = 128
CONV_WIDTH = 31
GDN_HEADS = 8
GDN_DIM = 128
GDN_CHUNK = 64
GDN_SHORT_CONV = 4

LANES = 128
SUBLANES = 8
MXU_DIM = 256
VMEM_LIMIT = 56 * 1024 * 1024
MASK_VALUE = -1e30
LOG2_E = 1.4426950408889634


def _params(n_axes):
    return pltpu.CompilerParams(dimension_semantics=("arbitrary",) * n_axes,
                                vmem_limit_bytes=VMEM_LIMIT)


def _resident(shape):
    nd = len(shape)
    return pl.BlockSpec(shape, lambda *_: (0,) * nd, pipeline_mode=pl.Buffered(1))


def _rms_bf16(x, g):
    ms = jnp.mean(x * x, axis=-1, keepdims=True)
    return (x * lax.rsqrt(ms + NORM_EPS) * g).astype(BF16)


def _silu(x):
    return x * jax.nn.sigmoid(x)


def _dot(a, b):
    return jnp.dot(a, b, preferred_element_type=F32)


def _dot_nt(a, b):
    return lax.dot_general(a, b, (((1,), (1,)), ((), ())), preferred_element_type=F32)


def _dot_tn(a, b):
    return lax.dot_general(a, b, (((0,), (0,)), ((), ())), preferred_element_type=F32)


def _ffn_body(h_ref, g_ref, win_ref, wout_ref, fg_ref, *refs, ffn, chunks, final_norm, mixed):
    if mixed:
        mix_ref, wmix_ref, o_ref, xn_ref, acc_ref, hres_ref = refs
        hres_ref[...] = h_ref[...] + _dot(mix_ref[...], wmix_ref[...])
        h_ref = hres_ref
    else:
        o_ref, xn_ref, acc_ref = refs
    xn_ref[...] = _rms_bf16(h_ref[...], g_ref[...])
    for n, (c0, cw) in enumerate(chunks):
        xn = xn_ref[...]
        gate = _dot(xn, win_ref[:, c0:c0 + cw])
        up = _dot(xn, win_ref[:, ffn + c0:ffn + c0 + cw])
        part = _dot((_silu(gate) * up).astype(BF16), wout_ref[c0:c0 + cw, :])
        if n == 0:
            acc_ref[...] = part
        else:
            acc_ref[...] += part
    out = h_ref[...] + 0.5 * acc_ref[...]
    if final_norm:
        ms = jnp.mean(out * out, axis=-1, keepdims=True)
        out = out * lax.rsqrt(ms + NORM_EPS) * fg_ref[...]
    o_ref[...] = out


def _ffn_chunks(ffn, width):
    out, c0 = [], 0
    while c0 < ffn:
        cw = min(width, ffn - c0)
        out.append((c0, cw))
        c0 += cw
    return tuple(out)


def _ffn(h, g, w_in, w_out, final_g, *, final_norm, tm, mix=None):
    t, d = h.shape
    ffn = w_out.shape[0]
    body = functools.partial(_ffn_body, ffn=ffn, chunks=_ffn_chunks(ffn, 4 * MXU_DIM),
                             final_norm=final_norm, mixed=mix is not None)
    row = lambda width: pl.BlockSpec((tm, width), lambda i: (i, 0))
    in_specs = [row(d), _resident((1, d)), _resident(w_in.shape), _resident(w_out.shape),
                _resident((1, d))]
    scratch = [pltpu.VMEM((tm, d), BF16), pltpu.VMEM((tm, d), F32)]
    args = [h, g, w_in, w_out, final_g]
    if mix is not None:
        in_specs += [row(mix[0].shape[1]), _resident(mix[1].shape)]
        scratch.append(pltpu.VMEM((tm, d), F32))
        args += list(mix)
    return pl.pallas_call(
        body,
        grid=(t // tm,),
        in_specs=in_specs,
        out_specs=row(d),
        out_shape=jax.ShapeDtypeStruct((t, d), F32),
        scratch_shapes=scratch,
        compiler_params=_params(1),
        name="ffn_final" if final_norm else ("ffn_mix" if mix is not None else "ffn"),
    )(*args)


def _hyb_in_body(h_ref, g_ref, w_ref, pos_ref, invf_ref, shi_ref, slo_ref, *refs, tm):
    nd = len(DILATIONS)
    qkv_refs = [refs[3 * n:3 * n + 3] for n in range(nd)]
    glu_ref, stage = refs[3 * nd], refs[3 * nd + 1]
    xn = _rms_bf16(h_ref[...], g_ref[...])
    proj = _dot(xn, w_ref[...])
    ang = pos_ref[...].astype(F32) * invf_ref[...]
    cos, sin = jnp.cos(ang), jnp.sin(ang)
    reps = A_WIDTH // LANES
    cos4 = jnp.concatenate([cos] * reps, axis=1)
    sin_hi = jnp.concatenate([sin * shi_ref[...]] * reps, axis=1)
    sin_lo = jnp.concatenate([sin * slo_ref[...]] * reps, axis=1)
    half = ROT_DIM // 2

    def rot(x):
        return (x * cos4 + pltpu.roll(x, half, 1) * sin_hi
                + pltpu.roll(x, A_WIDTH - half, 1) * sin_lo)

    qkv = (rot(proj[:, :A_WIDTH]) * (A_HEAD_DIM ** -0.5 * LOG2_E),
           rot(proj[:, A_WIDTH:2 * A_WIDTH]),
           proj[:, 2 * A_WIDTH:3 * A_WIDTH])
    u = proj[:, 3 * A_WIDTH:]
    c = u.shape[1] // 2
    glu_ref[...] = u[:, :c] * jax.nn.sigmoid(u[:, c:])

    slabs = A_WIDTH // LANES
    for a, x in enumerate(qkv):
        for s in range(slabs):
            stage[a * slabs + s] = x[:, s * LANES:(s + 1) * LANES]
    for dil, out_refs in zip(DILATIONS, qkv_refs):
        for a, out_ref in enumerate(out_refs):
            if dil == 1:
                out_ref[...] = qkv[a].astype(BF16)
                continue
            for cls in range(dil):
                for s in range(slabs):
                    piece = stage[a * slabs + s, pl.ds(cls, tm // dil, stride=dil), :]
                    lane0 = cls * A_WIDTH + s * LANES
                    out_ref[:, lane0:lane0 + LANES] = piece.astype(BF16)


def _hyb_in(h, g, w, pos, invf, shi, slo, *, tm):
    t, d = h.shape
    cc = (w.shape[1] - 3 * A_WIDTH) // 2
    row = lambda width: pl.BlockSpec((tm, width), lambda i: (i, 0))
    dil_specs, dil_shapes = [], []
    for dil in DILATIONS:
        dil_specs += [pl.BlockSpec((tm // dil, dil * A_WIDTH), lambda i: (i, 0))] * 3
        dil_shapes += [jax.ShapeDtypeStruct((t // dil, dil * A_WIDTH), BF16)] * 3
    outs = pl.pallas_call(
        functools.partial(_hyb_in_body, tm=tm),
        grid=(t // tm,),
        in_specs=[row(d), _resident((1, d)), _resident(w.shape), row(1),
                  _resident((1, LANES)), _resident((1, LANES)), _resident((1, LANES))],
        out_specs=dil_specs + [row(cc)],
        out_shape=dil_shapes + [jax.ShapeDtypeStruct((t, cc), F32)],
        scratch_shapes=[pltpu.VMEM((3 * A_WIDTH // LANES, tm, LANES), F32)],
        compiler_params=_params(1),
        name="hyb_in",
    )(h, g, w, pos, invf, shi, slo)
    n = len(DILATIONS)
    return [tuple(outs[3 * i:3 * i + 3]) for i in range(n)], outs[3 * n]


def _attn_body(q_ref, kc_ref, kp_ref, vc_ref, vp_ref, bias_ref, o_ref, lse_ref, kcat, vcat, *,
               tq, nc):
    i = pl.program_id(2)
    w = WIN_BLOCK
    kcat[0:w, :] = kp_ref[0]
    kcat[w:, :] = kc_ref[0]
    vcat[0:w, :] = vp_ref[0]
    vcat[w:, :] = vc_ref[0]
    slab = MXU_DIM
    hps = slab // A_HEAD_DIM
    lane_head = lax.broadcasted_iota(jnp.int32, (1, slab), 1) // A_HEAD_DIM
    lane_idx = lax.broadcasted_iota(jnp.int32, (1, LANES), 1)
    for cls in range(nc):
        for j in range(tq // w):
            first = jnp.where(jnp.logical_and(i == 0, j == 0), 1, 0) if j == 0 else 0
            bias = bias_ref[first]
            lse_tile = jnp.zeros((w, LANES), F32)
            for s in range(A_WIDTH // slab):
                cols = slice(cls * A_WIDTH + s * slab, cls * A_WIDTH + (s + 1) * slab)
                qs = q_ref[0, j * w:(j + 1) * w, cols]
                lhs = jnp.concatenate(
                    [jnp.where(lane_head == hh, qs, jnp.zeros_like(qs)) for hh in range(hps)],
                    axis=0)
                sc = _dot_nt(lhs, kcat[j * w:(j + 2) * w, cols]) + bias
                m = jnp.max(sc, axis=1, keepdims=True)
                p = jnp.exp2(sc - m)
                den = jnp.sum(p, axis=1, keepdims=True)
                pv = _dot(p.astype(BF16), vcat[j * w:(j + 2) * w, cols]) * (1.0 / den)
                lse = m + jnp.log2(den)
                o_slab = jnp.zeros((w, slab), F32)
                for hh in range(hps):
                    rows = slice(hh * w, (hh + 1) * w)
                    o_slab = o_slab + jnp.where(lane_head == hh, pv[rows], 0.0)
                    lse_tile = lse_tile + jnp.where(lane_idx == s * hps + hh, lse[rows], 0.0)
                o_ref[0, j * w:(j + 1) * w, cols] = o_slab
            lse_ref[0, j * w:(j + 1) * w, cls * LANES:(cls + 1) * LANES] = lse_tile


def _attn_bias():
    w = WIN_BLOCK
    r = jnp.arange(w)[:, None]
    c = jnp.arange(2 * w)[None, :]
    band = (c >= r) & (c <= r + w)
    no_prev = band & (c >= w)
    bias = jnp.where(jnp.stack([band, no_prev]), 0.0, MASK_VALUE).astype(F32)
    return jnp.tile(bias, (1, MXU_DIM // A_HEAD_DIM, 1))


def _attn_branch(q, k, v, bias, dil, batch):
    width = A_WIDTH
    b = batch
    l = q.shape[0] // b
    w = WIN_BLOCK
    rows_per_step = 4 * w
    tq = min(rows_per_step, l)
    nc = min(dil, rows_per_step // tq)
    view = lambda a: a.reshape(b, l, dil * width)
    cur = pl.BlockSpec((1, tq, nc * width), lambda bi, c, i: (bi, i, c))
    prev = pl.BlockSpec((1, w, nc * width),
                        lambda bi, c, i: (bi, jnp.maximum(i * (tq // w) - 1, 0), c))
    o, lse = pl.pallas_call(
        functools.partial(_attn_body, tq=tq, nc=nc),
        grid=(b, dil // nc, l // tq),
        in_specs=[cur, cur, prev, cur, prev, _resident(bias.shape)],
        out_specs=[cur, pl.BlockSpec((1, tq, nc * LANES), lambda bi, c, i: (bi, i, c))],
        out_shape=[jax.ShapeDtypeStruct((b, l, dil * width), F32),
                   jax.ShapeDtypeStruct((b, l, dil * LANES), F32)],
        scratch_shapes=[pltpu.VMEM((tq + w, nc * width), BF16),
                        pltpu.VMEM((tq + w, nc * width), BF16)],
        compiler_params=_params(3),
        name=f"attn_d{dil}",
    )(view(q), view(k), view(k), view(v), view(v), bias)
    return o.reshape(b * l, dil * width), lse.reshape(b * l, dil * LANES)


def _conv_block(xpad, dww_ref, r0, nrows, lanes, halo):
    base = r0 + halo - SUBLANES
    acc = None
    for b in range(SUBLANES):
        z = None
        for a in range(-(-CONV_WIDTH // SUBLANES)):
            delay = SUBLANES * a + b
            if delay >= CONV_WIDTH:
                continue
            j = CONV_WIDTH - 1 - delay
            start = base - SUBLANES * a
            term = xpad[start:start + nrows + SUBLANES, lanes] * dww_ref[j:j + 1, lanes]
            z = term if z is None else z + term
        zs = z if b == 0 else pltpu.roll(z, b, 0)
        acc = zs if acc is None else acc + zs
    return acc[SUBLANES:]


def _hyb_out_body(h_ref, o1_ref, o2_ref, o3_ref, l1_ref, l2_ref, l3_ref, glu_ref, gprev_ref,
                  dww_ref, dwb_ref, lng_ref, lnb_ref, wout_ref, out_ref,
                  xpad, ostage, lstage, cat, *, tm, halo, tiles_per_seq, rb, sub):
    i = pl.program_id(0)
    slabs = A_WIDTH // LANES
    for n, (dil, o_ref, l_ref) in enumerate(zip(DILATIONS[1:], (o2_ref, o3_ref), (l2_ref, l3_ref))):
        for cls in range(dil):
            rows = pl.ds(cls, tm // dil, stride=dil)
            for s in range(slabs):
                lane0 = cls * A_WIDTH + s * LANES
                ostage[n * slabs + s, rows, :] = o_ref[:, lane0:lane0 + LANES]
            lstage[n, rows, :] = l_ref[:, cls * LANES:(cls + 1) * LANES]

    prev = gprev_ref[...]
    xpad[0:halo, :] = jnp.where(i % tiles_per_seq == 0, jnp.zeros_like(prev), prev)
    xpad[halo:, :] = glu_ref[...]

    lane = lax.broadcasted_iota(jnp.int32, (1, LANES), 1)
    heads_per_vreg = LANES // A_HEAD_DIM
    for blk in range(tm // rb):
        rows = slice(blk * rb, (blk + 1) * rb)
        l1, l2, l3 = l1_ref[rows, :], lstage[0, rows, :], lstage[1, rows, :]
        mx = jnp.maximum(jnp.maximum(l1, l2), l3)
        e1, e2, e3 = jnp.exp2(l1 - mx), jnp.exp2(l2 - mx), jnp.exp2(l3 - mx)
        tot = e1 + e2 + e3
        inv_tot = 1.0 / tot
        ws = (e1 * inv_tot, e2 * inv_tot, e3 * inv_tot)
        pieces = []
        for s in range(slabs):
            o_gs = (o1_ref[rows, s * LANES:(s + 1) * LANES], ostage[s, rows, :],
                    ostage[slabs + s, rows, :])
            acc = None
            for wg, o_g in zip(ws, o_gs):
                wfull = wg[:, s * heads_per_vreg:s * heads_per_vreg + 1]
                for hh in range(1, heads_per_vreg):
                    col = s * heads_per_vreg + hh
                    wfull = jnp.where(lane < hh * A_HEAD_DIM, wfull, wg[:, col:col + 1])
                term = wfull * o_g
                acc = term if acc is None else acc + term
            pieces.append(acc)

        cw = glu_ref.shape[1]
        conv = jnp.concatenate(
            [_conv_block(xpad, dww_ref, blk * rb, rb, slice(s * LANES, (s + 1) * LANES), halo)
             for s in range(cw // LANES)], axis=1) + dwb_ref[...]
        mu = jnp.mean(conv, axis=-1, keepdims=True)
        xc = conv - mu
        y = xc * lax.rsqrt(jnp.mean(xc * xc, axis=-1, keepdims=True) + NORM_EPS)
        c = _silu(y * lng_ref[...] + lnb_ref[...])
        cat[rows, :] = jnp.concatenate(pieces + [c], axis=1).astype(BF16)

        if (blk + 1) % (sub // rb) == 0:
            srows = slice((blk + 1) * rb - sub, (blk + 1) * rb)
            out_ref[srows, :] = h_ref[srows, :] + _dot(cat[srows, :], wout_ref[...])


def _hyb_out(h, os_, lses, glu, dw_w, dw_b, ln_g, ln_b, w_out, *, tm, seq):
    t, d = h.shape
    cc = glu.shape[1]
    halo = 4 * SUBLANES
    assert DILATIONS[0] == 1 and halo >= SUBLANES * -(-CONV_WIDTH // SUBLANES)
    row = lambda width: pl.BlockSpec((tm, width), lambda i: (i, 0))
    dil_row = lambda width: [pl.BlockSpec((tm // dil, dil * width), lambda i: (i, 0))
                             for dil in DILATIONS]
    prev = pl.BlockSpec((halo, cc), lambda i: (jnp.maximum(i * (tm // halo) - 1, 0), 0))
    body = functools.partial(_hyb_out_body, tm=tm, halo=halo, tiles_per_seq=seq // tm,
                             rb=16 * SUBLANES, sub=min(tm, MXU_DIM))
    nd = len(DILATIONS) - 1
    return pl.pallas_call(
        body,
        grid=(t // tm,),
        in_specs=[row(d)] + dil_row(A_WIDTH) + dil_row(LANES) + [row(cc), prev,
                  _resident(dw_w.shape), _resident((1, cc)), _resident((1, cc)),
                  _resident((1, cc)), _resident(w_out.shape)],
        out_specs=row(d),
        out_shape=jax.ShapeDtypeStruct((t, d), F32),
        scratch_shapes=[pltpu.VMEM((tm + halo, cc), F32),
                        pltpu.VMEM((nd * A_WIDTH // LANES, tm, LANES), F32),
                        pltpu.VMEM((nd, tm, LANES), F32),
                        pltpu.VMEM((tm, d), BF16)],
        compiler_params=_params(1),
        name="hyb_out",
    )(h, *os_, *lses, glu, glu, dw_w, dw_b, ln_g, ln_b, w_out)


def _gdn_in_body(h_ref, g_ref, w_ref, ws_ref, cw_ref, alog_ref, dtb_ref,
                 qkv_ref, z_ref, bg_ref, xn_ref, carry, *, tm, tiles_per_seq, chunk, rb):
    i = pl.program_id(0)
    xn_ref[...] = _rms_bf16(h_ref[...], g_ref[...])
    hw = GDN_HEADS * GDN_DIM
    halo = SUBLANES
    taps = GDN_SHORT_CONV

    @pl.when(i % tiles_per_seq == 0)
    def _():
        carry[...] = jnp.zeros_like(carry)

    def l2n(x):
        return x * lax.rsqrt(jnp.sum(x * x, axis=-1, keepdims=True) + NORM_EPS)

    def finish_qkv(c0, pc):
        for s0 in range(0, chunk, GDN_DIM):
            lanes = slice(c0 + s0, c0 + s0 + GDN_DIM)
            w0, w1, w2, w3 = (cw_ref[j:j + 1, lanes] for j in range(taps))
            for r0 in range(0, tm, rb):
                top = carry[:, lanes] if r0 == 0 else pc[r0 - halo:r0, s0:s0 + GDN_DIM]
                ext = jnp.concatenate([top, pc[r0:r0 + rb, s0:s0 + GDN_DIM]], axis=0)
                x1 = pltpu.roll(ext, 1, 0)
                even = ext * w3 + x1 * w2
                odd = ext * w1 + x1 * w0
                y = _silu((even + pltpu.roll(odd, 2, 0))[halo:])
                qkv_ref[r0:r0 + rb, lanes] = l2n(y) if c0 < 2 * hw else y
            carry[:, lanes] = pc[tm - halo:, s0:s0 + GDN_DIM]

    qkv_chunks = list(range(0, 3 * hw, chunk))
    z_chunks = list(range(3 * hw, 4 * hw, chunk))
    every = -(-len(qkv_chunks) // max(len(z_chunks), 1))
    order = []
    for n, c0 in enumerate(qkv_chunks):
        order.append(c0)
        if (n + 1) % every == 0 and z_chunks:
            order.append(z_chunks.pop(0))
    order += z_chunks
    pending = None
    for c0 in order + [None]:
        nxt = None if c0 is None else _dot(xn_ref[...], w_ref[:, c0:c0 + chunk])
        if pending is not None:
            p0, pc = pending
            if p0 < 3 * hw:
                finish_qkv(p0, pc)
            else:
                z_ref[:, p0 - 3 * hw:p0 - 3 * hw + chunk] = pc
        pending = (c0, nxt)

    small = _dot(xn_ref[...], ws_ref[...])
    x = small + dtb_ref[...]
    softplus = jnp.maximum(x, 0.0) + jnp.log1p(jnp.exp(-jnp.abs(x)))
    lane = lax.broadcasted_iota(jnp.int32, (1, LANES), 1)
    bg_ref[...] = jnp.where(lane < GDN_HEADS, jax.nn.sigmoid(small),
                            -jnp.exp(alog_ref[...]) * softplus)


def _gdn_in(h, g, w_main, w_small, conv_w, alog, dtb, *, tm, seq):
    t, d = h.shape
    hw = GDN_HEADS * GDN_DIM
    row = lambda width: pl.BlockSpec((tm, width), lambda i: (i, 0))
    body = functools.partial(_gdn_in_body, tm=tm, tiles_per_seq=seq // tm, chunk=2 * MXU_DIM,
                             rb=8 * SUBLANES)
    return pl.pallas_call(
        body,
        grid=(t // tm,),
        in_specs=[row(d), _resident((1, d)),
                  _resident(w_main.shape), _resident(w_small.shape), _resident(conv_w.shape),
                  _resident((1, LANES)), _resident((1, LANES))],
        out_specs=[row(3 * hw), row(hw), row(LANES)],
        out_shape=[jax.ShapeDtypeStruct((t, 3 * hw), F32), jax.ShapeDtypeStruct((t, hw), F32),
                   jax.ShapeDtypeStruct((t, LANES), F32)],
        scratch_shapes=[pltpu.VMEM((tm, d), BF16), pltpu.VMEM((SUBLANES, 3 * hw), F32)],
        compiler_params=_params(1),
        name="gdn_in",
    )(h, g, w_main, w_small, conv_w, alog, dtb)


def _split2(a):
    hi = a.astype(BF16)
    lo = (a - hi.astype(F32)).astype(BF16)
    return hi, lo


def _split_dot(a_parts, b_parts):
    a_hi, a_lo = a_parts
    b_hi, b_lo = b_parts
    return _dot(jnp.concatenate([a_hi, a_hi, a_lo], axis=1),
                jnp.concatenate([b_hi, b_lo, b_hi], axis=0))


def _unit_lower_inverses(lms, eye, nilpotency):
    n = lms[0].shape[0]
    pows = [-lm for lm in lms]
    sums = [eye + p for p in pows]
    pows = [_split_dot(ps, ps) for ps in (_split2(p) for p in pows)]
    m = 2
    while 2 * m < nilpotency:
        prods = []
        for p, s in zip(pows, sums):
            (p_hi, p_lo), (s_hi, s_lo) = _split2(p), _split2(s)
            wide = (jnp.concatenate([p_hi, s_hi], axis=1), jnp.concatenate([p_lo, s_lo], axis=1))
            prods.append(_split_dot((p_hi, p_lo), wide))
        sums = [s + r[:, n:] for s, r in zip(sums, prods)]
        pows = [r[:, :n] for r in prods]
        m *= 2
    return [s + _split_dot(_split2(p), _split2(s)) for s, p in zip(sums, pows)]


def _block_diag(a, b):
    zero = jnp.zeros_like(a)
    return jnp.concatenate([jnp.concatenate([a, zero], axis=1),
                            jnp.concatenate([zero, b], axis=1)], axis=0)


def _gdn_core_body(qkv_ref, z_ref, bg_ref, ng_ref, o_ref, state, *, tc):
    n = pl.program_id(1)
    c = GDN_CHUNK
    pr = 2 * c
    hw = GDN_HEADS * GDN_DIM
    dim = GDN_DIM
    scale = GDN_DIM ** -0.5
    head_pairs = GDN_HEADS // 2

    @pl.when(n == 0)
    def _():
        state[...] = jnp.zeros_like(state)

    ri = lax.broadcasted_iota(jnp.int32, (tc, tc), 0)
    ci = lax.broadcasted_iota(jnp.int32, (tc, tc), 1)
    tri_l = jnp.logical_and(ri // c == ci // c, ri >= ci).astype(BF16)
    bg = bg_ref[...]
    g1 = bg.astype(BF16)
    r1 = bg - g1.astype(F32)
    g2 = r1.astype(BF16)
    g3 = (r1 - g2.astype(F32)).astype(BF16)
    gc = _dot(jnp.concatenate([tri_l] * 3, axis=1), jnp.concatenate([g1, g2, g3], axis=0))
    gct = gc.T

    pi = lax.broadcasted_iota(jnp.int32, (pr, pr), 0)
    pj = lax.broadcasted_iota(jnp.int32, (pr, pr), 1)
    same_chunk = pi // c == pj // c
    eye_f = (pi == pj).astype(F32)
    causal = jnp.logical_and(same_chunk, pi >= pj)
    strict = jnp.logical_and(same_chunk, pi > pj)
    row_chunk = lax.broadcasted_iota(jnp.int32, (pr, 1), 0) // c

    for pp in range(tc // pr):
        lms, attns, rhs2s, q_decs, kd_ts = [], [], [], [], []
        rows = slice(pp * pr, (pp + 1) * pr)
        for hp in range(head_pairs):
            lhs1, ks, decays = [], [], []
            for hh in (2 * hp, 2 * hp + 1):
                q = qkv_ref[rows, hh * dim:(hh + 1) * dim] * scale
                k = qkv_ref[rows, hw + hh * dim:hw + (hh + 1) * dim]
                v = qkv_ref[rows, 2 * hw + hh * dim:2 * hw + (hh + 1) * dim]
                b_col = bg[rows, hh:hh + 1]
                g_col = gc[rows, GDN_HEADS + hh:GDN_HEADS + hh + 1]
                g_row = gct[GDN_HEADS + hh:GDN_HEADS + hh + 1, rows]
                decays.append(jnp.exp(jnp.where(causal, g_col - g_row, MASK_VALUE)))
                eg = jnp.exp(g_col)
                kb = k * b_col
                lhs1.append(jnp.concatenate([kb, q], axis=0).astype(BF16))
                ks.append(k.astype(BF16))
                rhs2s.append(jnp.concatenate([v * b_col, kb * eg], axis=1).astype(BF16))
                q_decs.append((q * eg).astype(BF16))
                kds = []
                for cc in range(2):
                    last = pp * pr + cc * c + c - 1
                    g_last = gc[last:last + 1, GDN_HEADS + hh:GDN_HEADS + hh + 1]
                    kd = jnp.where(row_chunk == cc, k * jnp.exp(g_last - g_col), 0.0)
                    kds.append((kd.T.astype(BF16), jnp.exp(g_last)))
                kd_ts.append(kds)
            p1 = _dot_nt(jnp.concatenate(lhs1, axis=1), _block_diag(ks[0], ks[1]))
            for e in range(2):
                blk = p1[:, e * pr:(e + 1) * pr]
                lms.append(jnp.where(strict, blk[:pr] * decays[e], 0.0))
                attns.append((blk[pr:] * decays[e]).astype(BF16))

        t_invs = _unit_lower_inverses(lms, eye_f, c)
        p2s = [_dot(t.astype(BF16), r) for t, r in zip(t_invs, rhs2s)]

        v_done = {hh: [] for hh in range(GDN_HEADS)}
        for cc in range(2):
            cr = slice(cc * c, (cc + 1) * c)
            orow = slice(pp * pr + cc * c, pp * pr + (cc + 1) * c)
            s_olds = [state[hh] for hh in range(GDN_HEADS)]
            p3s = []
            for hp in range(head_pairs):
                lhs = []
                for hh in (2 * hp, 2 * hp + 1):
                    unit = hh
                    w = p2s[unit][cr, dim:]
                    lhs.append(jnp.concatenate([w.astype(BF16), q_decs[unit][cr]], axis=0))
                s_bd = _block_diag(s_olds[2 * hp].astype(BF16), s_olds[2 * hp + 1].astype(BF16))
                p3s.append(_dot(jnp.concatenate(lhs, axis=1), s_bd))
            for hp in range(head_pairs):
                lhs, v_pairs = [], []
                for e, hh in enumerate((2 * hp, 2 * hp + 1)):
                    unit = hh
                    u = p2s[unit][:, :dim]
                    v_new = u[cr] - p3s[hp][:c, e * dim:(e + 1) * dim]
                    v_done[hh].append(v_new)
                    v_pairs.append(jnp.concatenate(v_done[hh] + [u[c:]] * (1 - cc),
                                                   axis=0).astype(BF16))
                    lhs.append(jnp.concatenate([attns[unit][cr], kd_ts[unit][cc][0]], axis=0))
                p45 = _dot(jnp.concatenate(lhs, axis=1), _block_diag(v_pairs[0], v_pairs[1]))
                for e, hh in enumerate((2 * hp, 2 * hp + 1)):
                    unit = hh
                    lanes = slice(e * dim, (e + 1) * dim)
                    state[hh] = s_olds[hh] * kd_ts[unit][cc][1] + p45[c:, lanes]
                    o = p3s[hp][c:, lanes] + p45[:c, lanes]
                    on = (o * lax.rsqrt(jnp.mean(o * o, axis=-1, keepdims=True) + NORM_EPS)
                          * ng_ref[...])
                    cols = slice(hh * dim, (hh + 1) * dim)
                    o_ref[orow, cols] = (on * _silu(z_ref[orow, cols])).astype(BF16)


def _gdn_core(qkv, z, bg, norm_g, *, batch, seq, tc):
    t, hw = z.shape
    tiles = seq // tc
    blk = lambda width: pl.BlockSpec((tc, width), lambda b, n: (b * tiles + n, 0))
    return pl.pallas_call(
        functools.partial(_gdn_core_body, tc=tc),
        grid=(batch, tiles),
        in_specs=[blk(3 * hw), blk(hw), blk(LANES), _resident((1, GDN_DIM))],
        out_specs=blk(hw),
        out_shape=jax.ShapeDtypeStruct((t, hw), BF16),
        scratch_shapes=[pltpu.VMEM((GDN_HEADS, GDN_DIM, GDN_DIM), F32)],
        compiler_params=_params(2),
        name="gdn_core",
    )(qkv, z, bg, norm_g)


def _tile(n, want):
    t = min(want, n)
    assert n % t == 0, (n, t)
    return t


def kernel(x, positions, ffn1_norm, ffn1_w_in, ffn1_w_out, mix_norm, ffn2_norm, ffn2_w_in,
           ffn2_w_out, hyb_w_in, hyb_dw_w, hyb_dw_b, hyb_ln_g, hyb_ln_b, hyb_w_out,
           gdn_w_in, gdn_conv_w, gdn_A_log, gdn_dt_bias, gdn_norm_g, gdn_w_out, final_norm):
    batch, seq, d = x.shape
    depth = ffn1_norm.shape[0]
    t = batch * seq
    h = x.reshape(t, d)
    pos = positions.reshape(t, 1)
    row = lambda a: a.reshape(1, -1).astype(F32)

    lane = jnp.arange(LANES) % A_HEAD_DIM
    inv_freq = jnp.power(jnp.float32(ROPE_THETA),
                         -jnp.arange(0, ROT_DIM, 2, dtype=F32) / ROT_DIM)
    half = ROT_DIM // 2
    invf = jnp.where(lane < ROT_DIM, inv_freq[lane % half], 0.0).astype(F32).reshape(1, LANES)
    shi = jnp.where((lane >= half) & (lane < ROT_DIM), 1.0, 0.0).astype(F32).reshape(1, LANES)
    slo = jnp.where(lane < half, -1.0, 0.0).astype(F32).reshape(1, LANES)
    bias = _attn_bias()

    gdn_hw = GDN_HEADS * GDN_DIM
    tm_ffn = _tile(t, 1024)
    tm_mix = _tile(seq, 512)
    tm_hyb = _tile(seq, 1024)
    final_g = row(final_norm)

    for layer in range(depth):
        i = layer // 2
        h = _ffn(h, row(ffn1_norm[layer]), ffn1_w_in[layer].astype(BF16),
                 ffn1_w_out[layer].astype(BF16), final_g, final_norm=False, tm=tm_ffn)
        if layer % 2 == 0:
            qkvs, glu = _hyb_in(h, row(mix_norm[layer]), hyb_w_in[i].astype(BF16), pos,
                                invf, shi, slo, tm=tm_hyb)
            outs, lses = [], []
            for dil, (q, k, v) in zip(DILATIONS, qkvs):
                o_g, lse_g = _attn_branch(q, k, v, bias, dil, batch)
                outs.append(o_g)
                lses.append(lse_g)
            h = _hyb_out(h, outs, lses, glu, hyb_dw_w[i], row(hyb_dw_b[i]), row(hyb_ln_g[i]),
                         row(hyb_ln_b[i]), hyb_w_out[i].astype(BF16), tm=tm_mix, seq=seq)
            mix = None
        else:
            w_in = gdn_w_in[i]
            w_main = w_in[:, :4 * gdn_hw].astype(BF16)
            w_small = jnp.pad(w_in[:, 4 * gdn_hw:],
                              ((0, 0), (0, LANES - 2 * GDN_HEADS))).astype(BF16)
            lane_pad = lambda a: jnp.pad(a.astype(F32), (GDN_HEADS, LANES - 2 * GDN_HEADS)
                                         ).reshape(1, LANES)
            qkv, z, bg = _gdn_in(h, row(mix_norm[layer]), w_main, w_small, gdn_conv_w[i],
                                 lane_pad(gdn_A_log[i]), lane_pad(gdn_dt_bias[i]),
                                 tm=tm_mix, seq=seq)
            o = _gdn_core(qkv, z, bg, row(gdn_norm_g[i]), batch=batch, seq=seq,
                          tc=_tile(seq, 4 * GDN_CHUNK))
            mix = (o, gdn_w_out[i].astype(BF16))
        h = _ffn(h, row(ffn2_norm[layer]), ffn2_w_in[layer].astype(BF16),
                 ffn2_w_out[layer].astype(BF16), final_g,
                 final_norm=(layer == depth - 1), tm=tm_ffn, mix=mix)
    return h.reshape(batch, seq, d)
```
